```python
import jax
import jax.numpy as jnp
from jax import lax
import numpy as np

D_MODEL = 1024
BATCH = 8
SEQ = 4096
DEPTH = 4

CHUNK = 64
N_MIXERS = 2
N_A_LAYERS = (DEPTH + 1) // 2
N_B_LAYERS = DEPTH // 2

GDN_QK_HEADS = 8
GDN_V_HEADS = 16
GDN_HEAD_DIM = 128
GDN_KEY_DIM = GDN_QK_HEADS * GDN_HEAD_DIM
GDN_VAL_DIM = GDN_V_HEADS * GDN_HEAD_DIM
GDN_CONV = 4
GDN_CONV_DIM = 2 * GDN_KEY_DIM + GDN_VAL_DIM
GDN_IN_DIM = GDN_CONV_DIM + GDN_VAL_DIM + 2 * GDN_V_HEADS

SB_HEADS = 16
SB_HEAD_DIM = 64
SB_DIM = SB_HEADS * SB_HEAD_DIM
SB_Q_BLOCK = 128

N_EXPERTS = 32
TOP_K = 4
D_FF = D_MODEL
SWIGLU_LIMIT = 7.0
SWIGLU_ALPHA = 1.702
MOE_BLOCK = 256

DN_ALPHA = (2 * DEPTH) ** 0.25
DN_BETA = (8 * DEPTH) ** -0.25
LN_EPS = 1e-5
RMS_EPS = 1e-6
L2_EPS = 1e-6

kernel_name = "hybrid_gdn_stickbreak_moe_deepnorm"


def layer_norm(x, g, b):
    xf = x.astype(jnp.float32)
    mu = jnp.mean(xf, axis=-1, keepdims=True)
    var = jnp.mean(jnp.square(xf - mu), axis=-1, keepdims=True)
    return ((xf - mu) * lax.rsqrt(var + LN_EPS) * g.astype(jnp.float32) + b.astype(jnp.float32)).astype(x.dtype)


def l2_normalize(t):
    tf = t.astype(jnp.float32)
    return tf * lax.rsqrt(jnp.sum(tf * tf, axis=-1, keepdims=True) + L2_EPS)


def causal_depthwise_conv(x, w):
    k, c = w.shape
    return lax.conv_general_dilated(
        x, w[:, None, :].astype(x.dtype), window_strides=(1,), padding=[(k - 1, 0)],
        dimension_numbers=('NWC', 'WIO', 'NWC'), feature_group_count=c)


def chunk_gated_delta_rule(q, k, v, g, beta):
    f32 = jnp.float32
    b, h, s, dk = q.shape
    dv = v.shape[-1]
    nc = s // CHUNK

    def chunks(t):
        return t.astype(f32).reshape(b, h, nc, CHUNK, *t.shape[3:])

    q, k, v, g, beta = chunks(q), chunks(k), chunks(v), chunks(g), chunks(beta)
    g = jnp.cumsum(g, axis=-1)
    pos = jnp.arange(CHUNK)
    incl = pos[:, None] >= pos[None, :]
    strict = pos[:, None] > pos[None, :]
    decay = jnp.exp(jnp.where(incl, g[..., :, None] - g[..., None, :], -jnp.inf))
    k_beta = k * beta[..., None]
    a_strict = jnp.where(strict, jnp.einsum('bhncd,bhnsd->bhncs', k_beta, k) * decay, 0.0)
    eye = jnp.broadcast_to(jnp.eye(CHUNK, dtype=f32), a_strict.shape)
    t_inv = lax.linalg.triangular_solve(a_strict, eye, left_side=True, lower=True, unit_diagonal=True)
    u = jnp.einsum('bhncs,bhnsd->bhncd', t_inv, v * beta[..., None])
    w = jnp.einsum('bhncs,bhnsd->bhncd', t_inv, k_beta * jnp.exp(g)[..., None])
    q_dec = q * jnp.exp(g)[..., None]
    attn = jnp.einsum('bhncd,bhnsd->bhncs', q, k) * decay
    g_last = g[..., -1:]
    k_dec = k * jnp.exp(g_last - g)[..., None]
    chunk_decay = jnp.exp(g_last[..., 0])

    def step(state, xs):
        q_c, k_c, u_c, w_c, a_c, d_c = xs
        v_new = u_c - jnp.einsum('bhcd,bhde->bhce', w_c, state)
        o_c = jnp.einsum('bhcd,bhde->bhce', q_c, state) + jnp.einsum('bhcs,bhse->bhce', a_c, v_new)
        state = state * d_c[..., None, None] + jnp.einsum('bhcd,bhce->bhde', k_c, v_new)
        return state, o_c

    xs = (jnp.moveaxis(q_dec, 2, 0), jnp.moveaxis(k_dec, 2, 0), jnp.moveaxis(u, 2, 0),
          jnp.moveaxis(w, 2, 0), jnp.moveaxis(attn, 2, 0), jnp.moveaxis(chunk_decay, 2, 0))
    state0 = jnp.zeros((b, h, dk, dv), f32)
    _, o = lax.scan(step, state0, xs)
    return jnp.moveaxis(o, 0, 2).reshape(b, h, s, dv)


def gated_deltanet(x, w_in, conv_w, a_log, dt_bias, norm_g, w_out):
    f32 = jnp.float32
    b, s, _ = x.shape
    proj = x @ w_in
    o1 = GDN_CONV_DIM
    o2 = o1 + GDN_VAL_DIM
    o3 = o2 + GDN_V_HEADS
    qkv, z, b_raw, a_raw = proj[..., :o1], proj[..., o1:o2], proj[..., o2:o3], proj[..., o3:]
    qkv = jax.nn.silu(causal_depthwise_conv(qkv, conv_w))
    q = qkv[..., :GDN_KEY_DIM].reshape(b, s, GDN_QK_HEADS, GDN_HEAD_DIM)
    k = qkv[..., GDN_KEY_DIM:2 * GDN_KEY_DIM].reshape(b, s, GDN_QK_HEADS, GDN_HEAD_DIM)
    v = qkv[..., 2 * GDN_KEY_DIM:].reshape(b, s, GDN_V_HEADS, GDN_HEAD_DIM)
    q = l2_normalize(q) * (GDN_HEAD_DIM ** -0.5)
    k = l2_normalize(k)
    rep = GDN_V_HEADS // GDN_QK_HEADS
    q = jnp.repeat(q, rep, axis=2)
    k = jnp.repeat(k, rep, axis=2)
    beta = jax.nn.sigmoid(b_raw.astype(f32))
    g = -jnp.exp(a_log.astype(f32)) * jax.nn.softplus(a_raw.astype(f32) + dt_bias.astype(f32))
    o = chunk_gated_delta_rule(jnp.swapaxes(q, 1, 2), jnp.swapaxes(k, 1, 2), jnp.swapaxes(v, 1, 2),
                               jnp.swapaxes(g, 1, 2), jnp.swapaxes(beta, 1, 2))
    o = jnp.swapaxes(o, 1, 2)
    zf = z.astype(f32).reshape(b, s, GDN_V_HEADS, GDN_HEAD_DIM)
    o = o * lax.rsqrt(jnp.mean(o * o, axis=-1, keepdims=True) + RMS_EPS) * norm_g.astype(f32) * jax.nn.silu(zf)
    return o.reshape(b, s, GDN_VAL_DIM).astype(x.dtype) @ w_out


def stick_breaking_attention(x, w_qkv, w_out):
    f32 = jnp.float32
    b, s, _ = x.shape
    qkv = x @ w_qkv

    def heads(t):
        return t.reshape(b, s, SB_HEADS, SB_HEAD_DIM).transpose(0, 2, 1, 3)

    q = heads(qkv[..., :SB_DIM])
    k = heads(qkv[..., SB_DIM:2 * SB_DIM])
    v = heads(qkv[..., 2 * SB_DIM:])
    scale = SB_HEAD_DIM ** -0.5
    outs = []
    for blk in range(s // SB_Q_BLOCK):
        t0 = blk * SB_Q_BLOCK
        t1 = t0 + SB_Q_BLOCK
        z = jnp.einsum('bhtd,bhsd->bhts', q[:, :, t0:t1], k[:, :, :t1]).astype(f32) * scale
        causal = jnp.arange(t1)[None, :] < (t0 + jnp.arange(SB_Q_BLOCK))[:, None]
        log_keep = jnp.where(causal, jax.nn.log_sigmoid(-z), 0.0)
        after = lax.cumsum(log_keep, axis=3, reverse=True) - log_keep
        wts = jnp.exp(jnp.where(causal, jax.nn.log_sigmoid(z) + after, -jnp.inf))
        outs.append(jnp.einsum('bhts,bhsd->bhtd', wts.astype(v.dtype), v[:, :, :t1]))
    o = jnp.concatenate(outs, axis=2)
    return o.transpose(0, 2, 1, 3).reshape(b, s, SB_DIM) @ w_out


def moe_ffn(x, w_router, b_router, w_gu, b_gu, w_down, b_down):
    b, s, d = x.shape
    n_tok = b * s
    n_pairs = n_tok * TOP_K
    xt = x.reshape(n_tok, d)
    logits = (xt @ w_router + b_router).astype(jnp.float32)
    top_logits, top_idx = lax.top_k(logits, TOP_K)
    gates = jax.nn.softmax(top_logits, axis=-1)
    flat_e = top_idx.reshape(-1)
    flat_tok = jnp.repeat(jnp.arange(n_tok, dtype=jnp.int32), TOP_K)
    flat_gate = gates.reshape(-1)
    order = jnp.argsort(flat_e)
    e_sorted, tok_sorted, gate_sorted = flat_e[order], flat_tok[order], flat_gate[order]
    counts = jnp.bincount(flat_e, length=N_EXPERTS)
    padded = (counts + MOE_BLOCK - 1) // MOE_BLOCK * MOE_BLOCK
    starts = jnp.cumsum(counts) - counts
    pad_ends = jnp.cumsum(padded)
    pad_starts = pad_ends - padded
    dest = pad_starts[e_sorted] + (jnp.arange(n_pairs) - starts[e_sorted])
    n_blocks = -(-n_pairs // MOE_BLOCK) + N_EXPERTS
    n_rows = n_blocks * MOE_BLOCK
    rows_tok = jnp.zeros((n_rows,), jnp.int32).at[dest].set(tok_sorted)
    rows_gate = jnp.zeros((n_rows,), x.dtype).at[dest].set(gate_sorted.astype(x.dtype))
    x_rows = xt[rows_tok].reshape(n_blocks, MOE_BLOCK, d)
    block_expert = jnp.minimum(
        jnp.searchsorted(pad_ends, jnp.arange(n_blocks) * MOE_BLOCK, side='right'), N_EXPERTS - 1)

    def expert_block(args):
        xb, e = args
        hgu = xb @ w_gu[e] + b_gu[e]
        gate = jnp.minimum(hgu[:, 0::2], SWIGLU_LIMIT)
        up = jnp.clip(hgu[:, 1::2], -SWIGLU_LIMIT, SWIGLU_LIMIT)
        act = (up + 1.0) * gate * jax.nn.sigmoid(SWIGLU_ALPHA * gate)
        return act @ w_down[e] + b_down[e]

    y_rows = lax.map(expert_block, (x_rows, block_expert)).reshape(n_rows, d)
    out = jnp.zeros((n_tok, d), x.dtype).at[rows_tok].add(y_rows * rows_gate[:, None])
    return out.reshape(b, s, d)


def setup_inputs(seed: int = 0) -> dict:
    key = jax.random.key(seed)
    ks = jax.random.split(key, 20)
    f32 = jnp.float32

    def nrm(k, shape, scale):
        return jax.random.normal(k, shape, f32) * scale

    na, nb, nl = N_A_LAYERS, N_B_LAYERS, DEPTH
    dt = jnp.exp(jax.random.uniform(ks[4], (na, GDN_V_HEADS), f32, np.log(1e-3), np.log(1e-1)))
    dt_bias = dt + jnp.log(-jnp.expm1(-dt))
    return {
        "x": nrm(ks[0], (BATCH, SEQ, D_MODEL), 1.0),
        "gdn_w_in": nrm(ks[1], (na, D_MODEL, GDN_IN_DIM), D_MODEL ** -0.5),
        "gdn_conv": nrm(ks[2], (na, GDN_CONV, GDN_CONV_DIM), GDN_CONV ** -0.5),
        "gdn_a_log": jnp.log(jax.random.uniform(ks[3], (na, GDN_V_HEADS), f32, 1.0, 16.0)),
        "gdn_dt_bias": dt_bias,
        "gdn_norm_g": 1.0 + nrm(ks[5], (na, GDN_HEAD_DIM), 0.02),
        "gdn_w_out": nrm(ks[6], (na, GDN_VAL_DIM, D_MODEL), GDN_VAL_DIM ** -0.5 * DN_BETA),
        "sb_w_qkv": nrm(ks[7], (nb, D_MODEL, 3 * SB_DIM), D_MODEL ** -0.5),
        "sb_w_out": nrm(ks[8], (nb, SB_DIM, D_MODEL), SB_DIM ** -0.5 * DN_BETA),
        "ln1_g": 1.0 + nrm(ks[9], (nl, D_MODEL), 0.02),
        "ln1_b": nrm(ks[10], (nl, D_MODEL), 0.02),
        "moe_w_router": nrm(ks[11], (nl, D_MODEL, N_EXPERTS), D_MODEL ** -0.5),
        "moe_b_router": nrm(ks[12], (nl, N_EXPERTS), 0.01),
        "moe_w_gu": nrm(ks[13], (nl, N_EXPERTS, D_MODEL, 2 * D_FF), D_MODEL ** -0.5),
        "moe_b_gu": nrm(ks[14], (nl, N_EXPERTS, 2 * D_FF), 0.02),
        "moe_w_down": nrm(ks[15], (nl, N_EXPERTS, D_FF, D_MODEL), D_FF ** -0.5 * DN_BETA),
        "moe_b_down": nrm(ks[16], (nl, N_EXPERTS, D_MODEL), 0.02),
        "ln2_g": 1.0 + nrm(ks[17], (nl, D_MODEL), 0.02),
        "ln2_b": nrm(ks[18], (nl, D_MODEL), 0.02),
    }


def reference(x, gdn_w_in, gdn_conv, gdn_a_log, gdn_dt_bias, gdn_norm_g, gdn_w_out,
              sb_w_qkv, sb_w_out, ln1_g, ln1_b, moe_w_router, moe_b_router,
              moe_w_gu, moe_b_gu, moe_w_down, moe_b_down, ln2_g, ln2_b):
    for i in range(DEPTH):
        j = i // N_MIXERS
        if i % N_MIXERS == 0:
            h = gated_deltanet(x, gdn_w_in[j], gdn_conv[j], gdn_a_log[j], gdn_dt_bias[j],
                               gdn_norm_g[j], gdn_w_out[j])
        else:
            h = stick_breaking_attention(x, sb_w_qkv[j], sb_w_out[j])
        x = layer_norm(DN_ALPHA * x + h, ln1_g[i], ln1_b[i])
        f = moe_ffn(x, moe_w_router[i], moe_b_router[i], moe_w_gu[i], moe_b_gu[i],
                    moe_w_down[i], moe_b_down[i])
        x = layer_norm(DN_ALPHA * x + f, ln2_g[i], ln2_b[i])
    return x
```

```python
import functools

import jax
import jax.numpy as jnp
from jax import lax
from jax.experimental import pallas as pl
from jax.experimental.pallas import tpu as pltpu

F32, BF16, I32 = jnp.float32, jnp.bfloat16, jnp.int32

CHUNK = 64
GDN_QK_HEADS = 8
GDN_V_HEADS = 16
GDN_HEAD_DIM = 128
GDN_KEY_DIM = GDN_QK_HEADS * GDN_HEAD_DIM
GDN_VAL_DIM = GDN_V_HEADS * GDN_HEAD_DIM
GDN_CONV_DIM = 2 * GDN_KEY_DIM + GDN_VAL_DIM
SB_HEADS = 16
SB_HEAD_DIM = 64
N_EXPERTS = 32
TOP_K = 4
MOE_BLOCK = 256
SWIGLU_LIMIT = 7.0
SWIGLU_ALPHA = 1.702
LN_EPS = 1e-5
RMS_EPS = 1e-6
L2_EPS = 1e-6

LANES = 128
VMEM_LIMIT_BYTES = 56 * 1024 * 1024


def _params(*sem):
    return pltpu.CompilerParams(dimension_semantics=sem, vmem_limit_bytes=VMEM_LIMIT_BYTES)


def _bdot(a, b):
    return jnp.dot(a.astype(BF16), b.astype(BF16), preferred_element_type=F32)


def _bdot_nt(a, b):
    return lax.dot_general(a.astype(BF16), b.astype(BF16), (((1,), (1,)), ((), ())),
                           preferred_element_type=F32)


def _sigmoid(x):
    return 1.0 / (1.0 + jnp.exp(-x))


def _softplus(x):
    return jnp.maximum(x, 0.0) + jnp.log1p(jnp.exp(-jnp.abs(x)))


def _mm_kernel(x_ref, w_ref, o_ref):
    o_ref[...] = _bdot(x_ref[...], w_ref[...]).astype(o_ref.dtype)


def _matmul(x, w, out_dtype, tm, tn):
    m, k = x.shape
    n = w.shape[1]
    tm, tn = min(tm, m), min(tn, n)
    return pl.pallas_call(
        _mm_kernel,
        grid=(m // tm, n // tn),
        in_specs=[pl.BlockSpec((tm, k), lambda i, j: (i, 0)),
                  pl.BlockSpec((k, tn), lambda i, j: (0, j))],
        out_specs=pl.BlockSpec((tm, tn), lambda i, j: (i, j)),
        out_shape=jax.ShapeDtypeStruct((m, n), out_dtype),
        compiler_params=_params("arbitrary", "arbitrary"),
        name="dense_proj",
    )(x, w)


def _gdn_gate_kernel(x_ref, w_ref, alog_ref, dtb_ref, tri_ref, bg_ref, gc_ref):
    p = _bdot(x_ref[...], w_ref[...])
    col = lax.broadcasted_iota(I32, p.shape, 1)
    beta = _sigmoid(p)
    g = -jnp.exp(alog_ref[...]) * _softplus(p + dtb_ref[...])
    bg = jnp.where(col < GDN_V_HEADS, beta, jnp.where(col < 2 * GDN_V_HEADS, g, 0.0))
    bg_ref[...] = bg
    hi = bg.astype(BF16)
    lo = (bg - hi.astype(F32)).astype(BF16)
    tri = tri_ref[...]
    gc_ref[...] = (jnp.dot(tri, hi, preferred_element_type=F32)
                   + jnp.dot(tri, lo, preferred_element_type=F32))


def _gdn_gates(x, w_ba, a_log, dt_bias, tm):
    m, d = x.shape
    tm = min(tm, m)
    pad = LANES - 2 * GDN_V_HEADS
    w = jnp.pad(w_ba, ((0, 0), (0, pad))).astype(BF16)
    alog = jnp.pad(a_log.astype(F32), (GDN_V_HEADS, pad))[None, :]
    dtb = jnp.pad(dt_bias.astype(F32), (GDN_V_HEADS, pad))[None, :]
    r = jnp.arange(tm)
    tri = ((r[:, None] // CHUNK == r[None, :] // CHUNK) & (r[:, None] >= r[None, :])).astype(BF16)
    return pl.pallas_call(
        _gdn_gate_kernel,
        grid=(m // tm,),
        in_specs=[pl.BlockSpec((tm, d), lambda i: (i, 0)),
                  pl.BlockSpec((d, LANES), lambda i: (0, 0)),
                  pl.BlockSpec((1, LANES), lambda i: (0, 0)),
                  pl.BlockSpec((1, LANES), lambda i: (0, 0)),
                  pl.BlockSpec((tm, tm), lambda i: (0, 0))],
        out_specs=[pl.BlockSpec((tm, LANES), lambda i: (i, 0)),
                   pl.BlockSpec((tm, LANES), lambda i: (i, 0))],
        out_shape=[jax.ShapeDtypeStruct((m, LANES), F32)] * 2,
        compiler_params=_params("arbitrary"),
        name="gdn_gates",
    )(x, w, alog, dtb, tri)


CONV_PREV_ROWS = 16


def _gdn_conv_kernel(cur_ref, prev_ref, w_ref, o_ref, ext_ref, *, tb, n_norm_blocks, n_q_blocks):
    t = pl.program_id(1)
    c = pl.program_id(2)
    prev = prev_ref[0].astype(F32)
    ext_ref[0:CONV_PREV_ROWS, :] = jnp.where(t > 0, prev, 0.0)
    ext_ref[CONV_PREV_ROWS:, :] = cur_ref[0].astype(F32)
    w = w_ref[...]
    taps = w.shape[0]
    acc = None
    for j in range(taps):
        xs = ext_ref[pl.ds(CONV_PREV_ROWS - (taps - 1) + j, tb), :]
        term = xs * w[j:j + 1, :]
        acc = term if acc is None else acc + term
    y = acc * _sigmoid(acc)

    @pl.when(c < n_norm_blocks)
    def _():
        scale = jnp.where(c < n_q_blocks, GDN_HEAD_DIM ** -0.5, 1.0)
        for hsl in range(y.shape[1] // GDN_HEAD_DIM):
            sl = slice(hsl * GDN_HEAD_DIM, (hsl + 1) * GDN_HEAD_DIM)
            yh = y[:, sl]
            inv = lax.rsqrt(jnp.sum(yh * yh, axis=-1, keepdims=True) + L2_EPS) * scale
            o_ref[0, :, sl] = (yh * inv).astype(o_ref.dtype)

    @pl.when(c >= n_norm_blocks)
    def _():
        o_ref[0] = y.astype(o_ref.dtype)


def _gdn_conv(proj, conv_w, tb, cb):
    b, s, _ = proj.shape
    tb = min(tb, s)
    kern = functools.partial(_gdn_conv_kernel, tb=tb, n_norm_blocks=2 * GDN_KEY_DIM // cb,
                             n_q_blocks=GDN_KEY_DIM // cb)
    ratio = tb // CONV_PREV_ROWS
    return pl.pallas_call(
        kern,
        grid=(b, s // tb, GDN_CONV_DIM // cb),
        in_specs=[pl.BlockSpec((1, tb, cb), lambda bi, t, c: (bi, t, c)),
                  pl.BlockSpec((1, CONV_PREV_ROWS, cb),
                               lambda bi, t, c: (bi, jnp.maximum(t * ratio - 1, 0), c)),
                  pl.BlockSpec((conv_w.shape[0], cb), lambda bi, t, c: (0, c))],
        out_specs=pl.BlockSpec((1, tb, cb), lambda bi, t, c: (bi, t, c)),
        out_shape=jax.ShapeDtypeStruct((b, s, GDN_CONV_DIM), BF16),
        scratch_shapes=[pltpu.VMEM((tb + CONV_PREV_ROWS, cb), F32)],
        compiler_params=_params("arbitrary", "arbitrary", "arbitrary"),
        name="gdn_conv",
    )(proj, proj, conv_w.astype(F32))


def _gdn_delta_kernel(q_ref, k_ref, v_ref, z_ref, beta_ref, gc_ref, gl_ref, gr_ref, ng_ref,
                      o_ref, s_ref, *, tb):
    @pl.when(pl.program_id(2) == 0)
    def _():
        s_ref[...] = jnp.zeros_like(s_ref)

    q = q_ref[0].astype(F32)
    k = k_ref[0].astype(F32)
    v = v_ref[0].astype(F32)
    beta = beta_ref[0, 0]
    gc = gc_ref[0, 0]
    gl = gl_ref[0, 0]
    gr = gr_ref[0, 0]

    ri = lax.broadcasted_iota(I32, (tb, tb), 0)
    ci = lax.broadcasted_iota(I32, (tb, tb), 1)
    same64 = (ri >> 6) == (ci >> 6)
    same32 = (ri >> 5) == (ci >> 5)
    same16 = (ri >> 4) == (ci >> 4)
    incl = same64 & (ri >= ci)
    strict = same64 & (ri > ci)
    eye = jnp.where(ri == ci, 1.0, 0.0)

    decay = jnp.where(incl, jnp.exp(jnp.minimum(gc - gr, 0.0)), 0.0)
    kb = k * beta
    a = jnp.where(strict, _bdot_nt(kb, k) * decay, 0.0)
    attn = _bdot_nt(q, k) * decay

    n1 = jnp.where(same16, -a, 0.0)
    inv = eye + n1
    n2 = _bdot(n1, n1)
    inv = inv + _bdot(inv, n2)
    n4 = _bdot(n2, n2)
    inv = inv + _bdot(inv, n4)
    n8 = _bdot(n4, n4)
    inv = inv + _bdot(inv, n8)
    a1 = jnp.where(same32 & jnp.logical_not(same16), a, 0.0)
    inv = inv - _bdot(_bdot(inv, a1), inv)
    a2 = jnp.where(same32, 0.0, a)
    inv = inv - _bdot(_bdot(inv, a2), inv)

    eg = jnp.exp(gc)
    rhs = jnp.concatenate([v * beta, kb * eg], axis=1)
    uw = rhs + _bdot(inv - eye, rhs)
    u = uw[:, :GDN_HEAD_DIM]
    w = uw[:, GDN_HEAD_DIM:]
    qd = q * eg
    kd = k * jnp.exp(gl - gc)

    state = s_ref[...]
    outs = []
    for c in range(tb // CHUNK):
        sl = slice(c * CHUNK, (c + 1) * CHUNK)
        sb = state.astype(BF16)
        v_new = u[sl] - _bdot(w[sl], sb)
        oc = _bdot(qd[sl], sb) + _bdot(attn[sl, sl], v_new)
        ds = _bdot(kd[sl].T, v_new)
        state = state * jnp.exp(gl[c * CHUNK:c * CHUNK + 1, :]) + ds
        outs.append(oc)
    s_ref[...] = state

    o = jnp.concatenate(outs, axis=0)
    rms = lax.rsqrt(jnp.mean(o * o, axis=-1, keepdims=True) + RMS_EPS)
    zf = z_ref[0].astype(F32)
    o_ref[0] = (o * rms * ng_ref[...] * (zf * _sigmoid(zf))).astype(o_ref.dtype)


def _gdn_delta(qkv, z_src, z_block0, beta_col, gc_col, gl_col, gc_row, norm_g, tb):
    b, s, _ = qkv.shape
    hd = GDN_HEAD_DIM
    rep = GDN_V_HEADS // GDN_QK_HEADS
    kb0 = GDN_KEY_DIM // hd
    vb0 = 2 * GDN_KEY_DIM // hd
    col_spec = pl.BlockSpec((1, 1, tb, 1), lambda bi, h, t: (bi, h, t, 0))
    return pl.pallas_call(
        functools.partial(_gdn_delta_kernel, tb=tb),
        grid=(b, GDN_V_HEADS, s // tb),
        in_specs=[pl.BlockSpec((1, tb, hd), lambda bi, h, t: (bi, t, h // rep)),
                  pl.BlockSpec((1, tb, hd), lambda bi, h, t: (bi, t, kb0 + h // rep)),
                  pl.BlockSpec((1, tb, hd), lambda bi, h, t: (bi, t, vb0 + h)),
                  pl.BlockSpec((1, tb, hd), lambda bi, h, t: (bi, t, z_block0 + h)),
                  col_spec, col_spec, col_spec,
                  pl.BlockSpec((1, 1, 1, tb), lambda bi, h, t: (bi, h, 0, t)),
                  pl.BlockSpec((1, hd), lambda bi, h, t: (0, 0))],
        out_specs=pl.BlockSpec((1, tb, hd), lambda bi, h, t: (bi, t, h)),
        out_shape=jax.ShapeDtypeStruct((b, s, GDN_VAL_DIM), BF16),
        scratch_shapes=[pltpu.VMEM((hd, hd), F32)],
        compiler_params=_params("arbitrary", "arbitrary", "arbitrary"),
        name="gdn_delta",
    )(qkv, qkv, qkv, z_src, beta_col, gc_col, gl_col, gc_row, norm_g.astype(F32)[None, :])


GDN_TIME_BLOCK = 256


def _gated_deltanet(x2d, b, s, w_in, conv_w, a_log, dt_bias, norm_g):
    o1 = GDN_CONV_DIM + GDN_VAL_DIM
    proj = _matmul(x2d, w_in[:, :o1].astype(BF16), BF16, 1024, 512)
    bg, gcum = _gdn_gates(x2d, w_in[:, o1:], a_log, dt_bias, 512)
    proj3 = proj.reshape(b, s, o1)
    qkv = _gdn_conv(proj3, conv_w, 1024, 512)
    nh = GDN_V_HEADS
    beta = bg[:, :nh].reshape(b, s, nh)
    gc = gcum[:, nh:2 * nh].reshape(b, s, nh)
    gl = jnp.repeat(gc.reshape(b, s // CHUNK, CHUNK, nh)[:, :, -1, :], CHUNK, axis=1)
    to_col = lambda a: jnp.swapaxes(a, 1, 2)[..., None]
    gc_row = jnp.swapaxes(gc, 1, 2)[:, :, None, :]
    tb = min(GDN_TIME_BLOCK, s)
    o = _gdn_delta(qkv, proj3, GDN_CONV_DIM // GDN_HEAD_DIM, to_col(beta), to_col(gc), to_col(gl),
                   gc_row, norm_g, tb)
    return o.reshape(b * s, GDN_VAL_DIM)


def _sb_kernel(q_ref, k_ref, v_ref, u_ref, o_ref, *, tq):
    i = pl.program_id(2)
    q = q_ref[0]
    lane = lax.broadcasted_iota(I32, q.shape, 1)
    q_heads = (jnp.where(lane < SB_HEAD_DIM, q, jnp.zeros_like(q)),
               jnp.where(lane >= SB_HEAD_DIM, q, jnp.zeros_like(q)))
    upper = u_ref[...]
    scale = SB_HEAD_DIM ** -0.5
    ri = lax.broadcasted_iota(I32, (tq, tq), 0)
    ci = lax.broadcasted_iota(I32, (tq, tq), 1)
    causal = ci < ri

    def visit(j, state, diagonal):
        kb = k_ref[0, pl.ds(pl.multiple_of(j * tq, tq), tq), :]
        vb = v_ref[0, pl.ds(pl.multiple_of(j * tq, tq), tq), :]
        new = []
        for qh, (carry, acc) in zip(q_heads, state):
            z = lax.dot_general(qh, kb, (((1,), (1,)), ((), ())), preferred_element_type=F32) * scale
            sp = _softplus(z)
            log_keep = -sp
            if diagonal:
                log_keep = jnp.where(causal, log_keep, 0.0)
            after = jnp.dot(log_keep.astype(BF16), upper, preferred_element_type=F32) + carry
            wts = jnp.exp(z - sp + after)
            if diagonal:
                wts = jnp.where(causal, wts, 0.0)
            acc = acc + jnp.dot(wts.astype(BF16), vb, preferred_element_type=F32)
            carry = carry + jnp.sum(log_keep, axis=-1, keepdims=True)
            new.append((carry, acc))
        return tuple(new)

    zero = (jnp.zeros((tq, 1), F32), jnp.zeros((tq, LANES), F32))
    state = visit(i, (zero, zero), True)
    state = lax.fori_loop(0, i, lambda jj, st: visit(i - 1 - jj, st, False), state)
    o_ref[0] = jnp.where(lane < SB_HEAD_DIM, state[0][1], state[1][1]).astype(o_ref.dtype)


SB_Q_TILE = 256


def _stick_breaking(qkv, b, s):
    tq = min(SB_Q_TILE, s)
    dim = SB_HEADS * SB_HEAD_DIM
    nb = dim // LANES
    r = jnp.arange(tq)
    upper = (r[:, None] > r[None, :]).astype(BF16)
    o = pl.pallas_call(
        functools.partial(_sb_kernel, tq=tq),
        grid=(b, nb, s // tq),
        in_specs=[pl.BlockSpec((1, tq, LANES), lambda bi, h, i: (bi, i, h)),
                  pl.BlockSpec((1, s, LANES), lambda bi, h, i: (bi, 0, nb + h)),
                  pl.BlockSpec((1, s, LANES), lambda bi, h, i: (bi, 0, 2 * nb + h)),
                  pl.BlockSpec((tq, tq), lambda bi, h, i: (0, 0))],
        out_specs=pl.BlockSpec((1, tq, LANES), lambda bi, h, i: (bi, i, h)),
        out_shape=jax.ShapeDtypeStruct((b, s, dim), BF16),
        compiler_params=_params("arbitrary", "arbitrary", "arbitrary"),
        name="stick_breaking",
    )(qkv, qkv, qkv, upper)
    return o.reshape(b * s, dim)


def _layer_norm(y, g, b):
    mu = jnp.mean(y, axis=-1, keepdims=True)
    d = y - mu
    var = jnp.mean(d * d, axis=-1, keepdims=True)
    return d * lax.rsqrt(var + LN_EPS) * g + b


def _proj_ln_router_kernel(h_ref, w_ref, x_ref, g_ref, b_ref, wrh_ref, wrl_ref, br_ref, tri_ref,
                           x1_ref, idx_ref, gate_ref, rank_ref, cnt_ref, carry_ref, *, alpha):
    @pl.when(pl.program_id(0) == 0)
    def _():
        carry_ref[...] = jnp.zeros_like(carry_ref)

    y = alpha * x_ref[...] + jnp.dot(h_ref[...], w_ref[...], preferred_element_type=F32)
    x1 = _layer_norm(y, g_ref[...], b_ref[...])
    x1_ref[...] = x1

    xh = x1.astype(BF16)
    xl = (x1 - xh.astype(F32)).astype(BF16)
    wrh = wrh_ref[...]
    logits = (jnp.dot(xh, wrh, preferred_element_type=F32)
              + jnp.dot(xl, wrh, preferred_element_type=F32)
              + jnp.dot(xh, wrl_ref[...], preferred_element_type=F32)) + br_ref[...]
    col = lax.broadcasted_iota(I32, logits.shape, 1)
    neg_inf = jnp.float32(-jnp.inf)
    work = jnp.where(col < N_EXPERTS, logits, neg_inf)
    tops, idxs = [], []
    onehot = jnp.zeros(logits.shape, F32)
    for _ in range(TOP_K):
        m = jnp.max(work, axis=-1, keepdims=True)
        idx = jnp.min(jnp.where(work == m, col, LANES), axis=-1, keepdims=True)
        sel = col == idx
        work = jnp.where(sel, neg_inf, work)
        onehot = jnp.where(sel, 1.0, onehot)
        tops.append(m)
        idxs.append(idx)
    exps = [jnp.exp(m - tops[0]) for m in tops]
    inv_den = 1.0 / functools.reduce(lambda p, r: p + r, exps)

    before = jnp.dot(tri_ref[...], onehot.astype(BF16), preferred_element_type=F32) + carry_ref[...]
    ranks = [jnp.sum(jnp.where(col == idx, before, 0.0), axis=-1, keepdims=True) for idx in idxs]
    carry_ref[...] = carry_ref[...] + jnp.sum(onehot, axis=0, keepdims=True)
    cnt_ref[...] = carry_ref[...]

    def slab(cols):
        out = jnp.zeros(logits.shape, cols[0].dtype)
        for kk, cv in enumerate(cols):
            out = jnp.where(col == kk, cv, out)
        return out[:, :TOP_K]

    idx_ref[...] = slab(idxs)
    gate_ref[...] = slab([e * inv_den for e in exps])
    rank_ref[...] = slab([r.astype(I32) for r in ranks])


def _proj_ln_router(h, w_out, x, ln_g, ln_b, w_router, b_router, alpha, tm):
    m, kh = h.shape
    d = x.shape[1]
    tm = min(tm, m)
    pad = LANES - N_EXPERTS
    wr = jnp.pad(w_router.astype(F32), ((0, 0), (0, pad)))
    wrh = wr.astype(BF16)
    wrl = (wr - wrh.astype(F32)).astype(BF16)
    br = jnp.pad(b_router.astype(F32), (0, pad))[None, :]
    r = jnp.arange(tm)
    tri = (r[:, None] > r[None, :]).astype(BF16)
    const = lambda i: (0, 0)
    row = lambda i: (i, 0)
    return pl.pallas_call(
        functools.partial(_proj_ln_router_kernel, alpha=alpha),
        grid=(m // tm,),
        in_specs=[pl.BlockSpec((tm, kh), row), pl.BlockSpec((kh, d), const), pl.BlockSpec((tm, d), row),
                  pl.BlockSpec((1, d), const), pl.BlockSpec((1, d), const),
                  pl.BlockSpec((d, LANES), const), pl.BlockSpec((d, LANES), const),
                  pl.BlockSpec((1, LANES), const), pl.BlockSpec((tm, tm), const)],
        out_specs=[pl.BlockSpec((tm, d), row), pl.BlockSpec((tm, TOP_K), row),
                   pl.BlockSpec((tm, TOP_K), row), pl.BlockSpec((tm, TOP_K), row),
                   pl.BlockSpec((1, LANES), const)],
        out_shape=[jax.ShapeDtypeStruct((m, d), F32), jax.ShapeDtypeStruct((m, TOP_K), I32),
                   jax.ShapeDtypeStruct((m, TOP_K), F32), jax.ShapeDtypeStruct((m, TOP_K), I32),
                   jax.ShapeDtypeStruct((1, LANES), F32)],
        scratch_shapes=[pltpu.VMEM((1, LANES), F32)],
        compiler_params=_params("arbitrary"),
        name="proj_ln_router",
    )(h, w_out.astype(BF16), x, ln_g.astype(F32)[None, :], ln_b.astype(F32)[None, :], wrh, wrl, br, tri)


def _row_gather_copy(src_hbm, row, dst, dst_row, sem):
    return pltpu.make_async_copy(src_hbm.at[pl.ds(row, 1), :], dst.at[pl.ds(dst_row, 1), :], sem)


def _expert_kernel(be_ref, nused_ref, rows0_ref, rows1_ref, x_hbm, wg_ref, wu_ref, bg_ref, bu_ref,
                   wd_ref, bd_ref, y_ref, xbuf, sem):
    i = pl.program_id(0)
    n_used = nused_ref[0]
    slot = i % 2
    nrows = xbuf.shape[1]

    def issue(rows_ref, dst_slot):
        def body(r, c):
            _row_gather_copy(x_hbm, rows_ref[0, 0, r], xbuf.at[dst_slot], r, sem.at[dst_slot]).start()
            return c
        lax.fori_loop(0, nrows, body, 0, unroll=8)

    @pl.when(i == 0)
    def _():
        issue(rows0_ref, 0)

    @pl.when(i < n_used)
    def _():
        pltpu.make_async_copy(x_hbm.at[pl.ds(0, nrows), :], xbuf.at[slot], sem.at[slot]).wait()

    @pl.when(i + 1 < n_used)
    def _():
        issue(rows1_ref, 1 - slot)

    @pl.when(i < n_used)
    def _():
        xb = xbuf[slot].astype(BF16)
        gate = jnp.dot(xb, wg_ref[0], preferred_element_type=F32) + bg_ref[0]
        up = jnp.dot(xb, wu_ref[0], preferred_element_type=F32) + bu_ref[0]
        gate = jnp.minimum(gate, SWIGLU_LIMIT)
        up = jnp.clip(up, -SWIGLU_LIMIT, SWIGLU_LIMIT)
        act = (up + 1.0) * gate * _sigmoid(SWIGLU_ALPHA * gate)
        y_ref[...] = jnp.dot(act.astype(BF16), wd_ref[0], preferred_element_type=F32) + bd_ref[0]

    @pl.when(i >= n_used)
    def _():
        y_ref[...] = jnp.zeros_like(y_ref)


def _experts(x1, rows_tok, block_expert, n_used, wg, wu, bg, bu, wd, bd):
    n_blocks = block_expert.shape[0]
    d = x1.shape[1]
    f = wg.shape[2]
    rows3 = rows_tok.reshape(n_blocks, 1, MOE_BLOCK)
    wspec = lambda shape: pl.BlockSpec(shape, lambda i, be, nu: (be[i], 0, 0))
    grid_spec = pltpu.PrefetchScalarGridSpec(
        num_scalar_prefetch=2,
        grid=(n_blocks,),
        in_specs=[pl.BlockSpec((1, 1, MOE_BLOCK), lambda i, be, nu: (i, 0, 0), memory_space=pltpu.SMEM),
                  pl.BlockSpec((1, 1, MOE_BLOCK), lambda i, be, nu: (jnp.minimum(i + 1, n_blocks - 1), 0, 0),
                               memory_space=pltpu.SMEM),
                  pl.BlockSpec(memory_space=pl.ANY),
                  wspec((1, d, f)), wspec((1, d, f)), wspec((1, 1, f)), wspec((1, 1, f)),
                  wspec((1, f, d)), wspec((1, 1, d))],
        out_specs=pl.BlockSpec((MOE_BLOCK, d), lambda i, be, nu: (i, 0)),
        scratch_shapes=[pltpu.VMEM((2, MOE_BLOCK, d), F32), pltpu.SemaphoreType.DMA((2,))],
    )
    return pl.pallas_call(
        _expert_kernel,
        grid_spec=grid_spec,
        out_shape=jax.ShapeDtypeStruct((n_blocks * MOE_BLOCK, d), F32),
        compiler_params=_params("arbitrary"),
        name="moe_experts",
    )(block_expert, n_used, rows3, rows3, x1, wg, wu, bg, bu, wd, bd)


def _combine_kernel(d0_ref, d1_ref, y_hbm, gate_ref, x1_ref, g_ref, b_ref, o_ref, ybuf, sem, *, alpha):
    i = pl.program_id(0)
    n = pl.num_programs(0)
    slot = i % 2
    tm = ybuf.shape[2]

    def issue(dest_ref, dst_slot):
        def body(r, c):
            for kk in range(TOP_K):
                _row_gather_copy(y_hbm, dest_ref[0, 0, r * TOP_K + kk], ybuf.at[dst_slot, kk], r,
                                 sem.at[dst_slot]).start()
            return c
        lax.fori_loop(0, tm, body, 0, unroll=2)

    @pl.when(i == 0)
    def _():
        issue(d0_ref, 0)

    for kk in range(TOP_K):
        pltpu.make_async_copy(y_hbm.at[pl.ds(0, tm), :], ybuf.at[slot, kk], sem.at[slot]).wait()

    @pl.when(i + 1 < n)
    def _():
        issue(d1_ref, 1 - slot)

    gates = gate_ref[...]
    f = None
    for kk in range(TOP_K):
        term = gates[:, kk:kk + 1] * ybuf[slot, kk]
        f = term if f is None else f + term
    o_ref[...] = _layer_norm(alpha * x1_ref[...] + f, g_ref[...], b_ref[...])


def _combine(y_rows, dest, gates, x1, ln_g, ln_b, alpha, tm):
    m, d = x1.shape
    tm = min(tm, m)
    nt = m // tm
    dest3 = dest.reshape(nt, 1, tm * TOP_K)
    row = lambda i: (i, 0)
    const = lambda i: (0, 0)
    return pl.pallas_call(
        functools.partial(_combine_kernel, alpha=alpha),
        grid=(nt,),
        in_specs=[pl.BlockSpec((1, 1, tm * TOP_K), lambda i: (i, 0, 0), memory_space=pltpu.SMEM),
                  pl.BlockSpec((1, 1, tm * TOP_K), lambda i: (jnp.minimum(i + 1, nt - 1), 0, 0),
                               memory_space=pltpu.SMEM),
                  pl.BlockSpec(memory_space=pl.ANY),
                  pl.BlockSpec((tm, TOP_K), row), pl.BlockSpec((tm, d), row),
                  pl.BlockSpec((1, d), const), pl.BlockSpec((1, d), const)],
        out_specs=pl.BlockSpec((tm, d), row),
        out_shape=jax.ShapeDtypeStruct((m, d), F32),
        scratch_shapes=[pltpu.VMEM((2, TOP_K, tm, d), F32), pltpu.SemaphoreType.DMA((2,))],
        compiler_params=_params("arbitrary"),
        name="moe_combine",
    )(dest3, dest3, y_rows, gates, x1, ln_g.astype(F32)[None, :], ln_b.astype(F32)[None, :])


def _moe_plan(top_idx, rank, counts):
    n_tok = top_idx.shape[0]
    n_pairs = n_tok * TOP_K
    cnt = counts[0, :N_EXPERTS].astype(I32)
    padded = (cnt + MOE_BLOCK - 1) // MOE_BLOCK * MOE_BLOCK
    pad_ends = jnp.cumsum(padded)
    pad_starts = pad_ends - padded
    dest = pad_starts[top_idx] + rank
    n_blocks = -(-n_pairs // MOE_BLOCK) + N_EXPERTS
    block_expert = jnp.minimum(
        jnp.searchsorted(pad_ends, jnp.arange(n_blocks, dtype=I32) * MOE_BLOCK, side='right'),
        N_EXPERTS - 1).astype(I32)
    n_used = (pad_ends[-1:] // MOE_BLOCK).astype(I32)
    flat_tok = jnp.repeat(jnp.arange(n_tok, dtype=I32), TOP_K)
    rows_tok = jnp.zeros((n_blocks * MOE_BLOCK,), I32).at[dest.reshape(-1)].set(flat_tok, unique_indices=True)
    return dest, rows_tok, block_expert, n_used


def kernel(x, gdn_w_in, gdn_conv, gdn_a_log, gdn_dt_bias, gdn_norm_g, gdn_w_out, sb_w_qkv, sb_w_out,
           ln1_g, ln1_b, moe_w_router, moe_b_router, moe_w_gu, moe_b_gu, moe_w_down, moe_b_down,
           ln2_g, ln2_b):
    b, s, d = x.shape
    depth = ln1_g.shape[0]
    alpha = float((2 * depth) ** 0.25)
    xt = x.reshape(b * s, d)
    for i in range(depth):
        j = i // 2
        if i % 2 == 0:
            h = _gated_deltanet(xt, b, s, gdn_w_in[j], gdn_conv[j], gdn_a_log[j], gdn_dt_bias[j],
                                gdn_norm_g[j])
            w_out = gdn_w_out[j]
        else:
            qkv = _matmul(xt, sb_w_qkv[j].astype(BF16), BF16, 1024, 512)
            h = _stick_breaking(qkv.reshape(b, s, -1), b, s)
            w_out = sb_w_out[j]
        x1, top_idx, gates, rank, counts = _proj_ln_router(
            h, w_out, xt, ln1_g[i], ln1_b[i], moe_w_router[i], moe_b_router[i], alpha, 512)
        dest, rows_tok, block_expert, n_used = _moe_plan(top_idx, rank, counts)
        wgu = moe_w_gu[i]
        bgu = moe_b_gu[i].astype(F32)
        y_rows = _experts(
            x1, rows_tok, block_expert, n_used,
            wgu[:, :, 0::2].astype(BF16), wgu[:, :, 1::2].astype(BF16),
            bgu[:, None, 0::2], bgu[:, None, 1::2],
            moe_w_down[i].astype(BF16), moe_b_down[i].astype(F32)[:, None, :])
        xt = _combine(y_rows, dest, gates, x1, ln2_g[i], ln2_b[i], alpha, 256)
    return xt.reshape(b, s, d)
```

```python
import functools

import jax
import jax.numpy as jnp
from jax import lax
from jax.experimental import pallas as pl
from jax.experimental.pallas import tpu as pltpu

F32, BF16, I32 = jnp.float32, jnp.bfloat16, jnp.int32

CHUNK = 64
GDN_QK_HEADS = 8
GDN_V_HEADS = 16
GDN_HEAD_DIM = 128
GDN_KEY_DIM = GDN_QK_HEADS * GDN_HEAD_DIM
GDN_VAL_DIM = GDN_V_HEADS * GDN_HEAD_DIM
GDN_CONV_DIM = 2 * GDN_KEY_DIM + GDN_VAL_DIM
SB_HEADS = 16
SB_HEAD_DIM = 64
N_EXPERTS = 32
TOP_K = 4
MOE_BLOCK = 256
SWIGLU_LIMIT = 7.0
SWIGLU_ALPHA = 1.702
LN_EPS = 1e-5
RMS_EPS = 1e-6
L2_EPS = 1e-6

LANES = 128
VMEM_LIMIT_BYTES = 56 * 1024 * 1024


def _params(*sem):
    return pltpu.CompilerParams(dimension_semantics=sem, vmem_limit_bytes=VMEM_LIMIT_BYTES)


def _bdot(a, b):
    return jnp.dot(a.astype(BF16), b.astype(BF16), preferred_element_type=F32)


def _bdot_nt(a, b):
    return lax.dot_general(a.astype(BF16), b.astype(BF16), (((1,), (1,)), ((), ())),
                           preferred_element_type=F32)


def _sigmoid(x):
    return 1.0 / (1.0 + jnp.exp(-x))


def _softplus(x):
    return jnp.maximum(x, 0.0) + jnp.log1p(jnp.exp(-jnp.abs(x)))


def _mm_kernel(x_ref, w_ref, o_ref):
    o_ref[...] = _bdot(x_ref[...], w_ref[...]).astype(o_ref.dtype)


def _matmul(x, w, out_dtype, tm, tn):
    m, k = x.shape
    n = w.shape[1]
    tm, tn = min(tm, m), min(tn, n)
    return pl.pallas_call(
        _mm_kernel,
        grid=(m // tm, n // tn),
        in_specs=[pl.BlockSpec((tm, k), lambda i, j: (i, 0)),
                  pl.BlockSpec((k, tn), lambda i, j: (0, j))],
        out_specs=pl.BlockSpec((tm, tn), lambda i, j: (i, j)),
        out_shape=jax.ShapeDtypeStruct((m, n), out_dtype),
        compiler_params=_params("arbitrary", "arbitrary"),
        name="dense_proj",
    )(x, w)


def _gdn_gate_kernel(x_ref, w_ref, alog_ref, dtb_ref, tri_ref, bg_ref, gc_ref):
    p = _bdot(x_ref[...], w_ref[...])
    col = lax.broadcasted_iota(I32, p.shape, 1)
    beta = _sigmoid(p)
    g = -jnp.exp(alog_ref[...]) * _softplus(p + dtb_ref[...])
    bg = jnp.where(col < GDN_V_HEADS, beta, jnp.where(col < 2 * GDN_V_HEADS, g, 0.0))
    bg_ref[...] = bg
    hi = bg.astype(BF16)
    lo = (bg - hi.astype(F32)).astype(BF16)
    tri = tri_ref[...]
    gc_ref[...] = (jnp.dot(tri, hi, preferred_element_type=F32)
                   + jnp.dot(tri, lo, preferred_element_type=F32))


def _gdn_gates(x, w_ba, a_log, dt_bias, tm):
    m, d = x.shape
    tm = min(tm, m)
    pad = LANES - 2 * GDN_V_HEADS
    w = jnp.pad(w_ba, ((0, 0), (0, pad))).astype(BF16)
    alog = jnp.pad(a_log.astype(F32), (GDN_V_HEADS, pad))[None, :]
    dtb = jnp.pad(dt_bias.astype(F32), (GDN_V_HEADS, pad))[None, :]
    r = jnp.arange(tm)
    tri = ((r[:, None] // CHUNK == r[None, :] // CHUNK) & (r[:, None] >= r[None, :])).astype(BF16)
    return pl.pallas_call(
        _gdn_gate_kernel,
        grid=(m // tm,),
        in_specs=[pl.BlockSpec((tm, d), lambda i: (i, 0)),
                  pl.BlockSpec((d, LANES), lambda i: (0, 0)),
                  pl.BlockSpec((1, LANES), lambda i: (0, 0)),
                  pl.BlockSpec((1, LANES), lambda i: (0, 0)),
                  pl.BlockSpec((tm, tm), lambda i: (0, 0))],
        out_specs=[pl.BlockSpec((tm, LANES), lambda i: (i, 0)),
                   pl.BlockSpec((tm, LANES), lambda i: (i, 0))],
        out_shape=[jax.ShapeDtypeStruct((m, LANES), F32)] * 2,
        compiler_params=_params("arbitrary"),
        name="gdn_gates",
    )(x, w, alog, dtb, tri)


CONV_PREV_ROWS = 16


def _gdn_conv_kernel(cur_ref, prev_ref, w_ref, o_ref, ext_ref, *, tb, n_norm_blocks, n_q_blocks):
    t = pl.program_id(1)
    c = pl.program_id(2)
    prev = prev_ref[0].astype(F32)
    ext_ref[0:CONV_PREV_ROWS, :] = jnp.where(t > 0, prev, 0.0)
    ext_ref[CONV_PREV_ROWS:, :] = cur_ref[0].astype(F32)
    w = w_ref[...]
    taps = w.shape[0]
    acc = None
    for j in range(taps):
        xs = ext_ref[pl.ds(CONV_PREV_ROWS - (taps - 1) + j, tb), :]
        term = xs * w[j:j + 1, :]
        acc = term if acc is None else acc + term
    y = acc * _sigmoid(acc)

    @pl.when(c < n_norm_blocks)
    def _():
        scale = jnp.where(c < n_q_blocks, GDN_HEAD_DIM ** -0.5, 1.0)
        for hsl in range(y.shape[1] // GDN_HEAD_DIM):
            sl = slice(hsl * GDN_HEAD_DIM, (hsl + 1) * GDN_HEAD_DIM)
            yh = y[:, sl]
            inv = lax.rsqrt(jnp.sum(yh * yh, axis=-1, keepdims=True) + L2_EPS) * scale
            o_ref[0, :, sl] = (yh * inv).astype(o_ref.dtype)

    @pl.when(c >= n_norm_blocks)
    def _():
        o_ref[0] = y.astype(o_ref.dtype)


def _gdn_conv(proj, conv_w, tb, cb):
    b, s, _ = proj.shape
    tb = min(tb, s)
    kern = functools.partial(_gdn_conv_kernel, tb=tb, n_norm_blocks=2 * GDN_KEY_DIM // cb,
                             n_q_blocks=GDN_KEY_DIM // cb)
    ratio = tb // CONV_PREV_ROWS
    return pl.pallas_call(
        kern,
        grid=(b, s // tb, GDN_CONV_DIM // cb),
        in_specs=[pl.BlockSpec((1, tb, cb), lambda bi, t, c: (bi, t, c)),
                  pl.BlockSpec((1, CONV_PREV_ROWS, cb),
                               lambda bi, t, c: (bi, jnp.maximum(t * ratio - 1, 0), c)),
                  pl.BlockSpec((conv_w.shape[0], cb), lambda bi, t, c: (0, c))],
        out_specs=pl.BlockSpec((1, tb, cb), lambda bi, t, c: (bi, t, c)),
        out_shape=jax.ShapeDtypeStruct((b, s, GDN_CONV_DIM), BF16),
        scratch_shapes=[pltpu.VMEM((tb + CONV_PREV_ROWS, cb), F32)],
        compiler_params=_params("arbitrary", "arbitrary", "arbitrary"),
        name="gdn_conv",
    )(proj, proj, conv_w.astype(F32))


GDN_KEY_HEADS_PER_STEP = 2


def _gdn_delta_kernel(q_ref, k_ref, v_ref, z_ref, cols_ref, gr_ref, ng_ref, o_ref, s_ref, *, tb):
    @pl.when(pl.program_id(2) == 0)
    def _():
        s_ref[...] = jnp.zeros_like(s_ref)

    hd = GDN_HEAD_DIM
    rep = GDN_V_HEADS // GDN_QK_HEADS
    ri = lax.broadcasted_iota(I32, (tb, tb), 0)
    ci = lax.broadcasted_iota(I32, (tb, tb), 1)
    same64 = (ri >> 6) == (ci >> 6)
    same32 = (ri >> 5) == (ci >> 5)
    same16 = (ri >> 4) == (ci >> 4)
    incl = same64 & (ri >= ci)
    strict = same64 & (ri > ci)
    pair32 = same32 & jnp.logical_not(same16)
    eye = jnp.where(ri == ci, 1.0, 0.0)
    ng = ng_ref[...]

    khs = range(GDN_KEY_HEADS_PER_STEP)
    vhs = range(GDN_KEY_HEADS_PER_STEP * rep)
    each = lambda fn: [fn(vh) for vh in vhs]

    q = [q_ref[0, :, kh * hd:(kh + 1) * hd].astype(F32) for kh in khs]
    k = [k_ref[0, :, kh * hd:(kh + 1) * hd].astype(F32) for kh in khs]
    kk = [_bdot_nt(k[kh], k[kh]) for kh in khs]
    qk = [_bdot_nt(q[kh], k[kh]) for kh in khs]
    cols = cols_ref[0, 0]
    nv = len(vhs)
    row = lax.broadcasted_iota(I32, (tb, 1), 0)
    beta = each(lambda vh: cols[:, vh:vh + 1])
    gc = each(lambda vh: cols[:, nv + vh:nv + vh + 1])

    def chunk_last(col):
        out = col[tb - 1:tb, :]
        for c in range(tb // CHUNK - 2, -1, -1):
            out = jnp.where(row < (c + 1) * CHUNK, col[(c + 1) * CHUNK - 1:(c + 1) * CHUNK, :], out)
        return out

    gl = each(lambda vh: chunk_last(gc[vh]))
    decay = each(lambda vh: jnp.where(incl, jnp.exp(jnp.minimum(gc[vh] - gr_ref[0, vh], 0.0)), 0.0))
    a = each(lambda vh: jnp.where(strict, beta[vh] * kk[vh // rep] * decay[vh], 0.0))
    attn = each(lambda vh: qk[vh // rep] * decay[vh])

    n1 = each(lambda vh: jnp.where(same16, -a[vh], 0.0))
    inv = each(lambda vh: eye + n1[vh])
    n2 = each(lambda vh: _bdot(n1[vh], n1[vh]))
    inv = each(lambda vh: inv[vh] + _bdot(inv[vh], n2[vh]))
    n4 = each(lambda vh: _bdot(n2[vh], n2[vh]))
    inv = each(lambda vh: inv[vh] + _bdot(inv[vh], n4[vh]))
    n8 = each(lambda vh: _bdot(n4[vh], n4[vh]))
    inv = each(lambda vh: inv[vh] + _bdot(inv[vh], n8[vh]))
    t1 = each(lambda vh: _bdot(inv[vh], jnp.where(pair32, a[vh], 0.0)))
    inv = each(lambda vh: inv[vh] - _bdot(t1[vh], inv[vh]))
    t2 = each(lambda vh: _bdot(inv[vh], jnp.where(same32, 0.0, a[vh])))
    inv = each(lambda vh: inv[vh] - _bdot(t2[vh], inv[vh]))

    eg = each(lambda vh: jnp.exp(gc[vh]))
    rhs = each(lambda vh: jnp.concatenate(
        [v_ref[0, :, vh * hd:(vh + 1) * hd].astype(F32) * beta[vh], k[vh // rep] * (beta[vh] * eg[vh])], axis=1))
    uw = each(lambda vh: rhs[vh] + _bdot(inv[vh] - eye, rhs[vh]))
    qd = each(lambda vh: q[vh // rep] * eg[vh])
    kd = each(lambda vh: k[vh // rep] * jnp.exp(gl[vh] - gc[vh]))

    state = each(lambda vh: s_ref[vh])
    outs = [[] for _ in vhs]
    for c in range(tb // CHUNK):
        sl = slice(c * CHUNK, (c + 1) * CHUNK)
        sb = each(lambda vh: state[vh].astype(BF16))
        v_new = each(lambda vh: uw[vh][sl, :hd] - _bdot(uw[vh][sl, hd:], sb[vh]))
        oc = each(lambda vh: _bdot(qd[vh][sl], sb[vh]) + _bdot(attn[vh][sl, sl], v_new[vh]))
        state = each(lambda vh: state[vh] * jnp.exp(gl[vh][c * CHUNK:c * CHUNK + 1, :])
                     + _bdot(kd[vh][sl].T, v_new[vh]))
        for vh in vhs:
            outs[vh].append(oc[vh])
    for vh in vhs:
        s_ref[vh] = state[vh]
        o = jnp.concatenate(outs[vh], axis=0)
        rms = lax.rsqrt(jnp.mean(o * o, axis=-1, keepdims=True) + RMS_EPS)
        zf = z_ref[0, :, vh * hd:(vh + 1) * hd].astype(F32)
        o_ref[0, :, vh * hd:(vh + 1) * hd] = (o * rms * ng * (zf * _sigmoid(zf))).astype(o_ref.dtype)


def _gdn_delta(qkv, z_src, z_col0, cols, gc_row, norm_g, tb):
    b, s, _ = qkv.shape
    hd = GDN_HEAD_DIM
    rep = GDN_V_HEADS // GDN_QK_HEADS
    kw = GDN_KEY_HEADS_PER_STEP * hd
    nv = GDN_KEY_HEADS_PER_STEP * rep
    vw = nv * hd
    return pl.pallas_call(
        functools.partial(_gdn_delta_kernel, tb=tb),
        grid=(b, GDN_QK_HEADS // GDN_KEY_HEADS_PER_STEP, s // tb),
        in_specs=[pl.BlockSpec((1, tb, kw), lambda bi, h, t: (bi, t, h)),
                  pl.BlockSpec((1, tb, kw), lambda bi, h, t: (bi, t, GDN_KEY_DIM // kw + h)),
                  pl.BlockSpec((1, tb, vw), lambda bi, h, t: (bi, t, 2 * GDN_KEY_DIM // vw + h)),
                  pl.BlockSpec((1, tb, vw), lambda bi, h, t: (bi, t, z_col0 // vw + h)),
                  pl.BlockSpec((1, 1, tb, LANES), lambda bi, h, t: (bi, h, t, 0)),
                  pl.BlockSpec((1, nv, 1, tb), lambda bi, h, t: (bi, h, 0, t)),
                  pl.BlockSpec((1, hd), lambda bi, h, t: (0, 0))],
        out_specs=pl.BlockSpec((1, tb, vw), lambda bi, h, t: (bi, t, h)),
        out_shape=jax.ShapeDtypeStruct((b, s, GDN_VAL_DIM), BF16),
        scratch_shapes=[pltpu.VMEM((nv, hd, hd), F32)],
        compiler_params=_params("arbitrary", "arbitrary", "arbitrary"),
        name="gdn_delta",
    )(qkv, qkv, qkv, z_src, cols, gc_row, norm_g.astype(F32)[None, :])


GDN_TIME_BLOCK = 256


def _gated_deltanet(x2d, b, s, w_in, conv_w, a_log, dt_bias, norm_g):
    o1 = GDN_CONV_DIM + GDN_VAL_DIM
    proj = _matmul(x2d, w_in[:, :o1].astype(BF16), BF16, 1024, 512)
    bg, gcum = _gdn_gates(x2d, w_in[:, o1:], a_log, dt_bias, 512)
    proj3 = proj.reshape(b, s, o1)
    qkv = _gdn_conv(proj3, conv_w, 1024, 512)
    nh = GDN_V_HEADS
    nv = GDN_KEY_HEADS_PER_STEP * (GDN_V_HEADS // GDN_QK_HEADS)
    beta = bg[:, :nh].reshape(b, s, nh // nv, nv)
    gc = gcum[:, nh:2 * nh].reshape(b, s, nh // nv, nv)
    cols = jnp.pad(jnp.concatenate([beta, gc], axis=-1), ((0, 0), (0, 0), (0, 0), (0, LANES - 2 * nv)))
    cols = jnp.swapaxes(cols, 1, 2)
    gc_row = jnp.swapaxes(gc.reshape(b, s, nh), 1, 2)[:, :, None, :]
    tb = min(GDN_TIME_BLOCK, s)
    o = _gdn_delta(qkv, proj3, GDN_CONV_DIM, cols, gc_row, norm_g, tb)
    return o.reshape(b * s, GDN_VAL_DIM)


def _sb_kernel(q_ref, k_ref, v_ref, u_ref, o_ref, *, tq):
    i = pl.program_id(2)
    q = q_ref[0] * jnp.asarray(SB_HEAD_DIM ** -0.5, q_ref.dtype)
    lane = lax.broadcasted_iota(I32, q.shape, 1)
    q_heads = (jnp.where(lane < SB_HEAD_DIM, q, jnp.zeros_like(q)),
               jnp.where(lane >= SB_HEAD_DIM, q, jnp.zeros_like(q)))
    upper = u_ref[...]
    ri = lax.broadcasted_iota(I32, (tq, tq), 0)
    ci = lax.broadcasted_iota(I32, (tq, tq), 1)
    causal = ci < ri

    def visit(j, state, diagonal):
        kb = k_ref[0, pl.ds(pl.multiple_of(j * tq, tq), tq), :]
        vb = v_ref[0, pl.ds(pl.multiple_of(j * tq, tq), tq), :]
        new = []
        for qh, (carry, acc) in zip(q_heads, state):
            z = lax.dot_general(qh, kb, (((1,), (1,)), ((), ())), preferred_element_type=F32)
            sp = _softplus(z)
            log_keep = -sp
            if diagonal:
                log_keep = jnp.where(causal, log_keep, 0.0)
            after = jnp.dot(log_keep.astype(BF16), upper, preferred_element_type=F32) + carry
            wts = jnp.exp(z - sp + after)
            if diagonal:
                wts = jnp.where(causal, wts, 0.0)
            acc = acc + jnp.dot(wts.astype(BF16), vb, preferred_element_type=F32)
            carry = carry + jnp.sum(log_keep, axis=-1, keepdims=True)
            new.append((carry, acc))
        return tuple(new)

    def alive(state):
        top = jnp.maximum(jnp.max(state[0][0]), jnp.max(state[1][0]))
        return (top > SB_LOG_UNDERFLOW).astype(I32)

    def body(st):
        jj, _, state = st
        state = visit(i - 1 - jj, state, False)
        return jj + 1, alive(state), state

    zero = (jnp.zeros((tq, 1), F32), jnp.zeros((tq, LANES), F32))
    state = visit(i, (zero, zero), True)
    _, _, state = lax.while_loop(lambda st: (st[0] < i) & (st[1] > 0), body, (jnp.int32(0), alive(state), state))
    o_ref[0] = jnp.where(lane < SB_HEAD_DIM, state[0][1], state[1][1]).astype(o_ref.dtype)


SB_Q_TILE = 256
SB_LOG_UNDERFLOW = -110.0


def _stick_breaking(qkv, b, s):
    tq = min(SB_Q_TILE, s)
    dim = SB_HEADS * SB_HEAD_DIM
    nb = dim // LANES
    r = jnp.arange(tq)
    upper = (r[:, None] > r[None, :]).astype(BF16)
    o = pl.pallas_call(
        functools.partial(_sb_kernel, tq=tq),
        grid=(b, nb, s // tq),
        in_specs=[pl.BlockSpec((1, tq, LANES), lambda bi, h, i: (bi, i, h)),
                  pl.BlockSpec((1, s, LANES), lambda bi, h, i: (bi, 0, nb + h)),
                  pl.BlockSpec((1, s, LANES), lambda bi, h, i: (bi, 0, 2 * nb + h)),
                  pl.BlockSpec((tq, tq), lambda bi, h, i: (0, 0))],
        out_specs=pl.BlockSpec((1, tq, LANES), lambda bi, h, i: (bi, i, h)),
        out_shape=jax.ShapeDtypeStruct((b, s, dim), BF16),
        compiler_params=_params("arbitrary", "arbitrary", "arbitrary"),
        name="stick_breaking",
    )(qkv, qkv, qkv, upper)
    return o.reshape(b * s, dim)


def _layer_norm(y, g, b):
    mu = jnp.mean(y, axis=-1, keepdims=True)
    d = y - mu
    var = jnp.mean(d * d, axis=-1, keepdims=True)
    return d * lax.rsqrt(var + LN_EPS) * g + b


def _proj_ln_router_kernel(h_ref, w_ref, x_ref, g_ref, b_ref, wrh_ref, wrl_ref, br_ref, tri_ref,
                           x1_ref, idx_ref, gate_ref, rank_ref, cnt_ref, carry_ref, *, alpha):
    @pl.when(pl.program_id(0) == 0)
    def _():
        carry_ref[...] = jnp.zeros_like(carry_ref)

    y = alpha * x_ref[...] + jnp.dot(h_ref[...], w_ref[...], preferred_element_type=F32)
    x1 = _layer_norm(y, g_ref[...], b_ref[...])
    x1_ref[...] = x1

    xh = x1.astype(BF16)
    xl = (x1 - xh.astype(F32)).astype(BF16)
    wrh = wrh_ref[...]
    logits = (jnp.dot(xh, wrh, preferred_element_type=F32)
              + jnp.dot(xl, wrh, preferred_element_type=F32)
              + jnp.dot(xh, wrl_ref[...], preferred_element_type=F32)) + br_ref[...]
    col = lax.broadcasted_iota(I32, logits.shape, 1)
    neg_inf = jnp.float32(-jnp.inf)
    work = jnp.where(col < N_EXPERTS, logits, neg_inf)
    tops, idxs = [], []
    onehot = jnp.zeros(logits.shape, F32)
    for _ in range(TOP_K):
        m = jnp.max(work, axis=-1, keepdims=True)
        idx = jnp.min(jnp.where(work == m, col, LANES), axis=-1, keepdims=True)
        sel = col == idx
        work = jnp.where(sel, neg_inf, work)
        onehot = jnp.where(sel, 1.0, onehot)
        tops.append(m)
        idxs.append(idx)
    exps = [jnp.exp(m - tops[0]) for m in tops]
    inv_den = 1.0 / functools.reduce(lambda p, r: p + r, exps)

    before = jnp.dot(tri_ref[...], onehot.astype(BF16), preferred_element_type=F32) + carry_ref[...]
    ranks = [jnp.sum(jnp.where(col == idx, before, 0.0), axis=-1, keepdims=True) for idx in idxs]
    carry_ref[...] = carry_ref[...] + jnp.sum(onehot, axis=0, keepdims=True)
    cnt_ref[...] = carry_ref[...]

    def slab(cols):
        out = jnp.zeros(logits.shape, cols[0].dtype)
        for kk, cv in enumerate(cols):
            out = jnp.where(col == kk, cv, out)
        return out[:, :TOP_K]

    idx_ref[...] = slab(idxs)
    gate_ref[...] = slab([e * inv_den for e in exps])
    rank_ref[...] = slab([r.astype(I32) for r in ranks])


def _proj_ln_router(h, w_out, x, ln_g, ln_b, w_router, b_router, alpha, tm):
    m, kh = h.shape
    d = x.shape[1]
    tm = min(tm, m)
    pad = LANES - N_EXPERTS
    wr = jnp.pad(w_router.astype(F32), ((0, 0), (0, pad)))
    wrh = wr.astype(BF16)
    wrl = (wr - wrh.astype(F32)).astype(BF16)
    br = jnp.pad(b_router.astype(F32), (0, pad))[None, :]
    r = jnp.arange(tm)
    tri = (r[:, None] > r[None, :]).astype(BF16)
    const = lambda i: (0, 0)
    row = lambda i: (i, 0)
    return pl.pallas_call(
        functools.partial(_proj_ln_router_kernel, alpha=alpha),
        grid=(m // tm,),
        in_specs=[pl.BlockSpec((tm, kh), row), pl.BlockSpec((kh, d), const), pl.BlockSpec((tm, d), row),
                  pl.BlockSpec((1, d), const), pl.BlockSpec((1, d), const),
                  pl.BlockSpec((d, LANES), const), pl.BlockSpec((d, LANES), const),
                  pl.BlockSpec((1, LANES), const), pl.BlockSpec((tm, tm), const)],
        out_specs=[pl.BlockSpec((tm, d), row), pl.BlockSpec((tm, TOP_K), row),
                   pl.BlockSpec((tm, TOP_K), row), pl.BlockSpec((tm, TOP_K), row),
                   pl.BlockSpec((1, LANES), const)],
        out_shape=[jax.ShapeDtypeStruct((m, d), F32), jax.ShapeDtypeStruct((m, TOP_K), I32),
                   jax.ShapeDtypeStruct((m, TOP_K), F32), jax.ShapeDtypeStruct((m, TOP_K), I32),
                   jax.ShapeDtypeStruct((1, LANES), F32)],
        scratch_shapes=[pltpu.VMEM((1, LANES), F32)],
        compiler_params=_params("arbitrary"),
        name="proj_ln_router",
    )(h, w_out.astype(BF16), x, ln_g.astype(F32)[None, :], ln_b.astype(F32)[None, :], wrh, wrl, br, tri)


def _row_gather_copy(src_hbm, row, dst, dst_row, sem):
    return pltpu.make_async_copy(src_hbm.at[pl.ds(row, 1), :], dst.at[pl.ds(dst_row, 1), :], sem)


GU_GROUP = 2 * LANES


def _deinterleave_kernel(w_ref, p_ref, o_ref):
    w = w_ref[0, 0].astype(BF16)
    perm = p_ref[...]
    for a in range(w.shape[1] // GU_GROUP):
        sl = slice(a * GU_GROUP, (a + 1) * GU_GROUP)
        o_ref[0, :, sl] = jnp.dot(w[:, sl], perm, preferred_element_type=F32).astype(o_ref.dtype)


def _deinterleave_gate_up(w_gu, layer, tk):
    _, e, d, f2 = w_gu.shape
    c = jnp.arange(GU_GROUP)
    src = jnp.where(c < LANES, 2 * c, 2 * (c - LANES) + 1)
    perm = (c[:, None] == src[None, :]).astype(BF16)
    return pl.pallas_call(
        _deinterleave_kernel,
        grid=(e, d // tk),
        in_specs=[pl.BlockSpec((1, 1, tk, f2), lambda ei, ki: (layer, ei, ki, 0)),
                  pl.BlockSpec((GU_GROUP, GU_GROUP), lambda ei, ki: (0, 0))],
        out_specs=pl.BlockSpec((1, tk, f2), lambda ei, ki: (ei, ki, 0)),
        out_shape=jax.ShapeDtypeStruct((e, d, f2), BF16),
        compiler_params=_params("arbitrary", "arbitrary"),
        name="gate_up_regroup",
    )(w_gu, perm)


def _regroup_gate_up_bias(b_gu):
    e, f2 = b_gu.shape
    b4 = b_gu.astype(F32).reshape(e, f2 // GU_GROUP, LANES, 2)
    return jnp.swapaxes(b4, 2, 3).reshape(e, 1, f2)


def _expert_kernel(be_ref, nused_ref, rows0_ref, rows1_ref, x_hbm, wgu_ref, bgu_ref,
                   wd_ref, bd_ref, y_ref, xbuf, sem):
    i = pl.program_id(0)
    n_used = nused_ref[0]
    slot = i % 2
    nrows = xbuf.shape[1]

    def issue(rows_ref, dst_slot):
        def body(r, c):
            _row_gather_copy(x_hbm, rows_ref[0, 0, r], xbuf.at[dst_slot], r, sem.at[dst_slot]).start()
            return c
        lax.fori_loop(0, nrows, body, 0, unroll=8)

    @pl.when(i == 0)
    def _():
        issue(rows0_ref, 0)

    @pl.when(i < n_used)
    def _():
        pltpu.make_async_copy(x_hbm.at[pl.ds(0, nrows), :], xbuf.at[slot], sem.at[slot]).wait()

    @pl.when(i + 1 < n_used)
    def _():
        issue(rows1_ref, 1 - slot)

    @pl.when(i < n_used)
    def _():
        xb = xbuf[slot].astype(BF16)
        hgu = jnp.dot(xb, wgu_ref[0], preferred_element_type=F32) + bgu_ref[0]
        acts = []
        for a in range(hgu.shape[1] // GU_GROUP):
            gate = jnp.minimum(hgu[:, a * GU_GROUP:a * GU_GROUP + LANES], SWIGLU_LIMIT)
            up = jnp.clip(hgu[:, a * GU_GROUP + LANES:(a + 1) * GU_GROUP], -SWIGLU_LIMIT, SWIGLU_LIMIT)
            acts.append(((up + 1.0) * gate * _sigmoid(SWIGLU_ALPHA * gate)).astype(BF16))
        act = jnp.concatenate(acts, axis=1)
        y_ref[...] = jnp.dot(act, wd_ref[0, 0].astype(BF16), preferred_element_type=F32) + bd_ref[0]

    @pl.when(i >= n_used)
    def _():
        y_ref[...] = jnp.zeros_like(y_ref)


def _experts(x1, rows_tok, block_expert, n_used, wgu, bgu, wd_all, layer, bd):
    n_blocks = block_expert.shape[0]
    d = x1.shape[1]
    f = wd_all.shape[2]
    rows3 = rows_tok.reshape(n_blocks, 1, MOE_BLOCK)
    wspec = lambda shape: pl.BlockSpec(shape, lambda i, be, nu: (be[i], 0, 0))
    wd_spec = pl.BlockSpec((1, 1, f, d), lambda i, be, nu: (layer, be[i], 0, 0))
    grid_spec = pltpu.PrefetchScalarGridSpec(
        num_scalar_prefetch=2,
        grid=(n_blocks,),
        in_specs=[pl.BlockSpec((1, 1, MOE_BLOCK), lambda i, be, nu: (i, 0, 0), memory_space=pltpu.SMEM),
                  pl.BlockSpec((1, 1, MOE_BLOCK), lambda i, be, nu: (jnp.minimum(i + 1, n_blocks - 1), 0, 0),
                               memory_space=pltpu.SMEM),
                  pl.BlockSpec(memory_space=pl.ANY),
                  wspec((1, d, 2 * f)), wspec((1, 1, 2 * f)), wd_spec, wspec((1, 1, d))],
        out_specs=pl.BlockSpec((MOE_BLOCK, d), lambda i, be, nu: (i, 0)),
        scratch_shapes=[pltpu.VMEM((2, MOE_BLOCK, d), F32), pltpu.SemaphoreType.DMA((2,))],
    )
    return pl.pallas_call(
        _expert_kernel,
        grid_spec=grid_spec,
        out_shape=jax.ShapeDtypeStruct((n_blocks * MOE_BLOCK, d), F32),
        compiler_params=_params("arbitrary"),
        name="moe_experts",
    )(block_expert, n_used, rows3, rows3, x1, wgu, bgu, wd_all, bd)


def _combine_kernel(d0_ref, d1_ref, y_hbm, gate_ref, x1_ref, g_ref, b_ref, o_ref, ybuf, sem, *, alpha):
    i = pl.program_id(0)
    n = pl.num_programs(0)
    slot = i % 2
    tm = ybuf.shape[2]

    def issue(dest_ref, dst_slot):
        def body(r, c):
            for kk in range(TOP_K):
                _row_gather_copy(y_hbm, dest_ref[0, 0, r * TOP_K + kk], ybuf.at[dst_slot, kk], r,
                                 sem.at[dst_slot]).start()
            return c
        lax.fori_loop(0, tm, body, 0, unroll=2)

    @pl.when(i == 0)
    def _():
        issue(d0_ref, 0)

    for kk in range(TOP_K):
        pltpu.make_async_copy(y_hbm.at[pl.ds(0, tm), :], ybuf.at[slot, kk], sem.at[slot]).wait()

    @pl.when(i + 1 < n)
    def _():
        issue(d1_ref, 1 - slot)

    gates = gate_ref[...]
    f = None
    for kk in range(TOP_K):
        term = gates[:, kk:kk + 1] * ybuf[slot, kk]
        f = term if f is None else f + term
    o_ref[...] = _layer_norm(alpha * x1_ref[...] + f, g_ref[...], b_ref[...])


def _combine(y_rows, dest, gates, x1, ln_g, ln_b, alpha, tm):
    m, d = x1.shape
    tm = min(tm, m)
    nt = m // tm
    dest3 = dest.reshape(nt, 1, tm * TOP_K)
    row = lambda i: (i, 0)
    const = lambda i: (0, 0)
    return pl.pallas_call(
        functools.partial(_combine_kernel, alpha=alpha),
        grid=(nt,),
        in_specs=[pl.BlockSpec((1, 1, tm * TOP_K), lambda i: (i, 0, 0), memory_space=pltpu.SMEM),
                  pl.BlockSpec((1, 1, tm * TOP_K), lambda i: (jnp.minimum(i + 1, nt - 1), 0, 0),
                               memory_space=pltpu.SMEM),
                  pl.BlockSpec(memory_space=pl.ANY),
                  pl.BlockSpec((tm, TOP_K), row), pl.BlockSpec((tm, d), row),
                  pl.BlockSpec((1, d), const), pl.BlockSpec((1, d), const)],
        out_specs=pl.BlockSpec((tm, d), row),
        out_shape=jax.ShapeDtypeStruct((m, d), F32),
        scratch_shapes=[pltpu.VMEM((2, TOP_K, tm, d), F32), pltpu.SemaphoreType.DMA((2,))],
        compiler_params=_params("arbitrary"),
        name="moe_combine",
    )(dest3, dest3, y_rows, gates, x1, ln_g.astype(F32)[None, :], ln_b.astype(F32)[None, :])


def _moe_plan(top_idx, rank, counts):
    n_tok = top_idx.shape[0]
    n_pairs = n_tok * TOP_K
    cnt = counts[0, :N_EXPERTS].astype(I32)
    padded = (cnt + MOE_BLOCK - 1) // MOE_BLOCK * MOE_BLOCK
    pad_ends = jnp.cumsum(padded)
    pad_starts = pad_ends - padded
    dest = pad_starts[top_idx] + rank
    n_blocks = -(-n_pairs // MOE_BLOCK) + N_EXPERTS
    block_row0 = jnp.arange(n_blocks, dtype=I32) * MOE_BLOCK
    block_expert = jnp.minimum(
        jnp.sum((pad_ends[None, :] <= block_row0[:, None]).astype(I32), axis=1), N_EXPERTS - 1)
    n_used = (pad_ends[-1:] // MOE_BLOCK).astype(I32)
    flat_tok = jnp.repeat(jnp.arange(n_tok, dtype=I32), TOP_K)
    rows_tok = jnp.zeros((n_blocks * MOE_BLOCK,), I32).at[dest.reshape(-1)].set(flat_tok, unique_indices=True)
    return dest, rows_tok, block_expert, n_used


def kernel(x, gdn_w_in, gdn_conv, gdn_a_log, gdn_dt_bias, gdn_norm_g, gdn_w_out, sb_w_qkv, sb_w_out,
           ln1_g, ln1_b, moe_w_router, moe_b_router, moe_w_gu, moe_b_gu, moe_w_down, moe_b_down,
           ln2_g, ln2_b):
    b, s, d = x.shape
    depth = ln1_g.shape[0]
    alpha = float((2 * depth) ** 0.25)
    xt = x.reshape(b * s, d)
    for i in range(depth):
        j = i // 2
        if i % 2 == 0:
            h = _gated_deltanet(xt, b, s, gdn_w_in[j], gdn_conv[j], gdn_a_log[j], gdn_dt_bias[j],
                                gdn_norm_g[j])
            w_out = gdn_w_out[j]
        else:
            qkv = _matmul(xt, sb_w_qkv[j].astype(BF16), BF16, 1024, 512)
            h = _stick_breaking(qkv.reshape(b, s, -1), b, s)
            w_out = sb_w_out[j]
        x1, top_idx, gates, rank, counts = _proj_ln_router(
            h, w_out, xt, ln1_g[i], ln1_b[i], moe_w_router[i], moe_b_router[i], alpha, 512)
        dest, rows_tok, block_expert, n_used = _moe_plan(top_idx, rank, counts)
        y_rows = _experts(
            x1, rows_tok, block_expert, n_used,
            _deinterleave_gate_up(moe_w_gu, i, 512), _regroup_gate_up_bias(moe_b_gu[i]),
            moe_w_down, i, moe_b_down[i].astype(F32)[:, None, :])
        xt = _combine(y_rows, dest, gates, x1, ln2_g[i], ln2_b[i], alpha, 256)
    return xt.reshape(b, s, d)
```

```python
import functools

import jax
import jax.numpy as jnp
from jax import lax
from jax.experimental import pallas as pl
from jax.experimental.pallas import tpu as pltpu

F32, BF16, I32 = jnp.float32, jnp.bfloat16, jnp.int32

CHUNK = 64
GDN_QK_HEADS = 8
GDN_V_HEADS = 16
GDN_HEAD_DIM = 128
GDN_KEY_DIM = GDN_QK_HEADS * GDN_HEAD_DIM
GDN_VAL_DIM = GDN_V_HEADS * GDN_HEAD_DIM
GDN_CONV_DIM = 2 * GDN_KEY_DIM + GDN_VAL_DIM
SB_HEADS = 16
SB_HEAD_DIM = 64
N_EXPERTS = 32
TOP_K = 4
MOE_BLOCK = 256
SWIGLU_LIMIT = 7.0
SWIGLU_ALPHA = 1.702
LN_EPS = 1e-5
RMS_EPS = 1e-6
L2_EPS = 1e-6

LANES = 128
VMEM_LIMIT_BYTES = 56 * 1024 * 1024


def _params(*sem):
    return pltpu.CompilerParams(dimension_semantics=sem, vmem_limit_bytes=VMEM_LIMIT_BYTES)


def _bdot(a, b):
    return jnp.dot(a.astype(BF16), b.astype(BF16), preferred_element_type=F32)


def _bdot_nt(a, b):
    return lax.dot_general(a.astype(BF16), b.astype(BF16), (((1,), (1,)), ((), ())),
                           preferred_element_type=F32)


def _sigmoid(x):
    return 1.0 / (1.0 + jnp.exp(-x))


def _softplus(x):
    return jnp.maximum(x, 0.0) + jnp.log1p(jnp.exp(-jnp.abs(x)))


def _mm_kernel(x_ref, w_ref, o_ref):
    o_ref[...] = _bdot(x_ref[...], w_ref[...]).astype(o_ref.dtype)


def _matmul(x, w, out_dtype, tm, tn):
    m, k = x.shape
    n = w.shape[1]
    tm, tn = min(tm, m), min(tn, n)
    return pl.pallas_call(
        _mm_kernel,
        grid=(m // tm, n // tn),
        in_specs=[pl.BlockSpec((tm, k), lambda i, j: (i, 0)),
                  pl.BlockSpec((k, tn), lambda i, j: (0, j))],
        out_specs=pl.BlockSpec((tm, tn), lambda i, j: (i, j)),
        out_shape=jax.ShapeDtypeStruct((m, n), out_dtype),
        compiler_params=_params("arbitrary", "arbitrary"),
        name="dense_proj",
    )(x, w)


def _gdn_gate_kernel(x_ref, w_ref, alog_ref, dtb_ref, tri_ref, bg_ref, gc_ref):
    p = _bdot(x_ref[...], w_ref[...])
    col = lax.broadcasted_iota(I32, p.shape, 1)
    beta = _sigmoid(p)
    g = -jnp.exp(alog_ref[...]) * _softplus(p + dtb_ref[...])
    bg = jnp.where(col < GDN_V_HEADS, beta, jnp.where(col < 2 * GDN_V_HEADS, g, 0.0))
    bg_ref[...] = bg
    hi = bg.astype(BF16)
    lo = (bg - hi.astype(F32)).astype(BF16)
    tri = tri_ref[...]
    gc_ref[...] = (jnp.dot(tri, hi, preferred_element_type=F32)
                   + jnp.dot(tri, lo, preferred_element_type=F32))


def _gdn_gates(x, w_ba, a_log, dt_bias, tm):
    m, d = x.shape
    tm = min(tm, m)
    pad = LANES - 2 * GDN_V_HEADS
    w = jnp.pad(w_ba, ((0, 0), (0, pad))).astype(BF16)
    alog = jnp.pad(a_log.astype(F32), (GDN_V_HEADS, pad))[None, :]
    dtb = jnp.pad(dt_bias.astype(F32), (GDN_V_HEADS, pad))[None, :]
    r = jnp.arange(tm)
    tri = ((r[:, None] // CHUNK == r[None, :] // CHUNK) & (r[:, None] >= r[None, :])).astype(BF16)
    return pl.pallas_call(
        _gdn_gate_kernel,
        grid=(m // tm,),
        in_specs=[pl.BlockSpec((tm, d), lambda i: (i, 0)),
                  pl.BlockSpec((d, LANES), lambda i: (0, 0)),
                  pl.BlockSpec((1, LANES), lambda i: (0, 0)),
                  pl.BlockSpec((1, LANES), lambda i: (0, 0)),
                  pl.BlockSpec((tm, tm), lambda i: (0, 0))],
        out_specs=[pl.BlockSpec((tm, LANES), lambda i: (i, 0)),
                   pl.BlockSpec((tm, LANES), lambda i: (i, 0))],
        out_shape=[jax.ShapeDtypeStruct((m, LANES), F32)] * 2,
        compiler_params=_params("arbitrary"),
        name="gdn_gates",
    )(x, w, alog, dtb, tri)


CONV_PREV_ROWS = 16


def _gdn_conv_kernel(cur_ref, prev_ref, w_ref, o_ref, ext_ref, *, tb, n_norm_blocks, n_q_blocks):
    t = pl.program_id(1)
    c = pl.program_id(2)
    prev = prev_ref[0].astype(F32)
    ext_ref[0:CONV_PREV_ROWS, :] = jnp.where(t > 0, prev, 0.0)
    ext_ref[CONV_PREV_ROWS:, :] = cur_ref[0].astype(F32)
    w = w_ref[...]
    taps = w.shape[0]
    acc = None
    for j in range(taps):
        xs = ext_ref[pl.ds(CONV_PREV_ROWS - (taps - 1) + j, tb), :]
        term = xs * w[j:j + 1, :]
        acc = term if acc is None else acc + term
    y = acc * _sigmoid(acc)

    @pl.when(c < n_norm_blocks)
    def _():
        scale = jnp.where(c < n_q_blocks, GDN_HEAD_DIM ** -0.5, 1.0)
        for hsl in range(y.shape[1] // GDN_HEAD_DIM):
            sl = slice(hsl * GDN_HEAD_DIM, (hsl + 1) * GDN_HEAD_DIM)
            yh = y[:, sl]
            inv = lax.rsqrt(jnp.sum(yh * yh, axis=-1, keepdims=True) + L2_EPS) * scale
            o_ref[0, :, sl] = (yh * inv).astype(o_ref.dtype)

    @pl.when(c >= n_norm_blocks)
    def _():
        o_ref[0] = y.astype(o_ref.dtype)


def _gdn_conv(proj, conv_w, tb, cb):
    b, s, _ = proj.shape
    tb = min(tb, s)
    kern = functools.partial(_gdn_conv_kernel, tb=tb, n_norm_blocks=2 * GDN_KEY_DIM // cb,
                             n_q_blocks=GDN_KEY_DIM // cb)
    ratio = tb // CONV_PREV_ROWS
    return pl.pallas_call(
        kern,
        grid=(b, s // tb, GDN_CONV_DIM // cb),
        in_specs=[pl.BlockSpec((1, tb, cb), lambda bi, t, c: (bi, t, c)),
                  pl.BlockSpec((1, CONV_PREV_ROWS, cb),
                               lambda bi, t, c: (bi, jnp.maximum(t * ratio - 1, 0), c)),
                  pl.BlockSpec((conv_w.shape[0], cb), lambda bi, t, c: (0, c))],
        out_specs=pl.BlockSpec((1, tb, cb), lambda bi, t, c: (bi, t, c)),
        out_shape=jax.ShapeDtypeStruct((b, s, GDN_CONV_DIM), BF16),
        scratch_shapes=[pltpu.VMEM((tb + CONV_PREV_ROWS, cb), F32)],
        compiler_params=_params("arbitrary", "arbitrary", "arbitrary"),
        name="gdn_conv",
    )(proj, proj, conv_w.astype(F32))


GDN_KEY_HEADS_PER_STEP = 4


def _gdn_delta_kernel(q_ref, k_ref, v_ref, z_ref, cols_ref, gr_ref, ng_ref, o_ref, s_ref, *, tb):
    @pl.when(pl.program_id(2) == 0)
    def _():
        s_ref[...] = jnp.zeros_like(s_ref)

    hd = GDN_HEAD_DIM
    rep = GDN_V_HEADS // GDN_QK_HEADS
    ri = lax.broadcasted_iota(I32, (tb, tb), 0)
    ci = lax.broadcasted_iota(I32, (tb, tb), 1)
    same64 = (ri >> 6) == (ci >> 6)
    same32 = (ri >> 5) == (ci >> 5)
    same16 = (ri >> 4) == (ci >> 4)
    incl = same64 & (ri >= ci)
    strict = same64 & (ri > ci)
    pair32 = same32 & jnp.logical_not(same16)
    eye = jnp.where(ri == ci, 1.0, 0.0)
    ng = ng_ref[...]

    khs = range(GDN_KEY_HEADS_PER_STEP)
    vhs = range(GDN_KEY_HEADS_PER_STEP * rep)
    each = lambda fn: [fn(vh) for vh in vhs]

    q = [q_ref[0, :, kh * hd:(kh + 1) * hd].astype(F32) for kh in khs]
    k = [k_ref[0, :, kh * hd:(kh + 1) * hd].astype(F32) for kh in khs]
    kk = [_bdot_nt(k[kh], k[kh]) for kh in khs]
    qk = [_bdot_nt(q[kh], k[kh]) for kh in khs]
    cols = cols_ref[0, 0]
    nv = len(vhs)
    row = lax.broadcasted_iota(I32, (tb, 1), 0)
    beta = each(lambda vh: cols[:, vh:vh + 1])
    gc = each(lambda vh: cols[:, nv + vh:nv + vh + 1])

    def chunk_last(col):
        out = col[tb - 1:tb, :]
        for c in range(tb // CHUNK - 2, -1, -1):
            out = jnp.where(row < (c + 1) * CHUNK, col[(c + 1) * CHUNK - 1:(c + 1) * CHUNK, :], out)
        return out

    gl = each(lambda vh: chunk_last(gc[vh]))
    decay = each(lambda vh: jnp.where(incl, jnp.exp(jnp.minimum(gc[vh] - gr_ref[0, vh], 0.0)), 0.0))
    a = each(lambda vh: jnp.where(strict, beta[vh] * kk[vh // rep] * decay[vh], 0.0))
    attn = each(lambda vh: qk[vh // rep] * decay[vh])

    n1 = each(lambda vh: jnp.where(same16, -a[vh], 0.0))
    inv = each(lambda vh: eye + n1[vh])
    n2 = each(lambda vh: _bdot(n1[vh], n1[vh]))
    inv = each(lambda vh: inv[vh] + _bdot(inv[vh], n2[vh]))
    n4 = each(lambda vh: _bdot(n2[vh], n2[vh]))
    inv = each(lambda vh: inv[vh] + _bdot(inv[vh], n4[vh]))
    n8 = each(lambda vh: _bdot(n4[vh], n4[vh]))
    inv = each(lambda vh: inv[vh] + _bdot(inv[vh], n8[vh]))
    t1 = each(lambda vh: _bdot(inv[vh], jnp.where(pair32, a[vh], 0.0)))
    inv = each(lambda vh: inv[vh] - _bdot(t1[vh], inv[vh]))
    t2 = each(lambda vh: _bdot(inv[vh], jnp.where(same32, 0.0, a[vh])))
    inv = each(lambda vh: inv[vh] - _bdot(t2[vh], inv[vh]))

    eg = each(lambda vh: jnp.exp(gc[vh]))
    rhs = each(lambda vh: jnp.concatenate(
        [v_ref[0, :, vh * hd:(vh + 1) * hd].astype(F32) * beta[vh], k[vh // rep] * (beta[vh] * eg[vh])], axis=1))
    uw = each(lambda vh: rhs[vh] + _bdot(inv[vh] - eye, rhs[vh]))
    qd = each(lambda vh: q[vh // rep] * eg[vh])
    kd = each(lambda vh: k[vh // rep] * jnp.exp(gl[vh] - gc[vh]))

    state = each(lambda vh: s_ref[vh])
    outs = [[] for _ in vhs]
    for c in range(tb // CHUNK):
        sl = slice(c * CHUNK, (c + 1) * CHUNK)
        sb = each(lambda vh: state[vh].astype(BF16))
        v_new = each(lambda vh: uw[vh][sl, :hd] - _bdot(uw[vh][sl, hd:], sb[vh]))
        oc = each(lambda vh: _bdot(qd[vh][sl], sb[vh]) + _bdot(attn[vh][sl, sl], v_new[vh]))
        state = each(lambda vh: state[vh] * jnp.exp(gl[vh][c * CHUNK:c * CHUNK + 1, :])
                     + _bdot(kd[vh][sl].T, v_new[vh]))
        for vh in vhs:
            outs[vh].append(oc[vh])
    for vh in vhs:
        s_ref[vh] = state[vh]
        o = jnp.concatenate(outs[vh], axis=0)
        rms = lax.rsqrt(jnp.mean(o * o, axis=-1, keepdims=True) + RMS_EPS)
        zf = z_ref[0, :, vh * hd:(vh + 1) * hd].astype(F32)
        o_ref[0, :, vh * hd:(vh + 1) * hd] = (o * rms * ng * (zf * _sigmoid(zf))).astype(o_ref.dtype)


def _gdn_delta(qkv, z_src, z_col0, cols, gc_row, norm_g, tb):
    b, s, _ = qkv.shape
    hd = GDN_HEAD_DIM
    rep = GDN_V_HEADS // GDN_QK_HEADS
    kw = GDN_KEY_HEADS_PER_STEP * hd
    nv = GDN_KEY_HEADS_PER_STEP * rep
    vw = nv * hd
    return pl.pallas_call(
        functools.partial(_gdn_delta_kernel, tb=tb),
        grid=(b, GDN_QK_HEADS // GDN_KEY_HEADS_PER_STEP, s // tb),
        in_specs=[pl.BlockSpec((1, tb, kw), lambda bi, h, t: (bi, t, h)),
                  pl.BlockSpec((1, tb, kw), lambda bi, h, t: (bi, t, GDN_KEY_DIM // kw + h)),
                  pl.BlockSpec((1, tb, vw), lambda bi, h, t: (bi, t, 2 * GDN_KEY_DIM // vw + h)),
                  pl.BlockSpec((1, tb, vw), lambda bi, h, t: (bi, t, z_col0 // vw + h)),
                  pl.BlockSpec((1, 1, tb, LANES), lambda bi, h, t: (bi, h, t, 0)),
                  pl.BlockSpec((1, nv, 1, tb), lambda bi, h, t: (bi, h, 0, t)),
                  pl.BlockSpec((1, hd), lambda bi, h, t: (0, 0))],
        out_specs=pl.BlockSpec((1, tb, vw), lambda bi, h, t: (bi, t, h)),
        out_shape=jax.ShapeDtypeStruct((b, s, GDN_VAL_DIM), BF16),
        scratch_shapes=[pltpu.VMEM((nv, hd, hd), F32)],
        compiler_params=_params("arbitrary", "arbitrary", "arbitrary"),
        name="gdn_delta",
    )(qkv, qkv, qkv, z_src, cols, gc_row, norm_g.astype(F32)[None, :])


GDN_TIME_BLOCK = 256


def _gated_deltanet(x2d, b, s, w_in, conv_w, a_log, dt_bias, norm_g):
    o1 = GDN_CONV_DIM + GDN_VAL_DIM
    proj = _matmul(x2d, w_in[:, :o1].astype(BF16), BF16, 1024, 512)
    bg, gcum = _gdn_gates(x2d, w_in[:, o1:], a_log, dt_bias, 512)
    proj3 = proj.reshape(b, s, o1)
    qkv = _gdn_conv(proj3, conv_w, 1024, 512)
    nh = GDN_V_HEADS
    nv = GDN_KEY_HEADS_PER_STEP * (GDN_V_HEADS // GDN_QK_HEADS)
    beta = bg[:, :nh].reshape(b, s, nh // nv, nv)
    gc = gcum[:, nh:2 * nh].reshape(b, s, nh // nv, nv)
    cols = jnp.pad(jnp.concatenate([beta, gc], axis=-1), ((0, 0), (0, 0), (0, 0), (0, LANES - 2 * nv)))
    cols = jnp.swapaxes(cols, 1, 2)
    gc_row = jnp.swapaxes(gc.reshape(b, s, nh), 1, 2)[:, :, None, :]
    tb = min(GDN_TIME_BLOCK, s)
    o = _gdn_delta(qkv, proj3, GDN_CONV_DIM, cols, gc_row, norm_g, tb)
    return o.reshape(b * s, GDN_VAL_DIM)


def _sb_kernel(q_ref, k_ref, v_ref, u_ref, o_ref, *, tq):
    i = pl.program_id(2)
    q = q_ref[0] * jnp.asarray(SB_HEAD_DIM ** -0.5, q_ref.dtype)
    lane = lax.broadcasted_iota(I32, q.shape, 1)
    q_heads = (jnp.where(lane < SB_HEAD_DIM, q, jnp.zeros_like(q)),
               jnp.where(lane >= SB_HEAD_DIM, q, jnp.zeros_like(q)))
    upper = u_ref[...]
    ri = lax.broadcasted_iota(I32, (tq, tq), 0)
    ci = lax.broadcasted_iota(I32, (tq, tq), 1)
    causal = ci < ri

    def visit(j, state, diagonal):
        kb = k_ref[0, pl.ds(pl.multiple_of(j * tq, tq), tq), :]
        vb = v_ref[0, pl.ds(pl.multiple_of(j * tq, tq), tq), :]
        new = []
        for qh, (carry, acc) in zip(q_heads, state):
            z = lax.dot_general(qh, kb, (((1,), (1,)), ((), ())), preferred_element_type=F32)
            sp = _softplus(z)
            log_keep = -sp
            if diagonal:
                log_keep = jnp.where(causal, log_keep, 0.0)
            after = jnp.dot(log_keep.astype(BF16), upper, preferred_element_type=F32) + carry
            wts = jnp.exp(z - sp + after)
            if diagonal:
                wts = jnp.where(causal, wts, 0.0)
            acc = acc + jnp.dot(wts.astype(BF16), vb, preferred_element_type=F32)
            carry = carry + jnp.sum(log_keep, axis=-1, keepdims=True)
            new.append((carry, acc))
        return tuple(new)

    def alive(state):
        top = jnp.maximum(jnp.max(state[0][0]), jnp.max(state[1][0]))
        return (top > SB_LOG_UNDERFLOW).astype(I32)

    def body(st):
        jj, _, state = st
        state = visit(i - 1 - jj, state, False)
        return jj + 1, alive(state), state

    zero = (jnp.zeros((tq, 1), F32), jnp.zeros((tq, LANES), F32))
    state = visit(i, (zero, zero), True)
    _, _, state = lax.while_loop(lambda st: (st[0] < i) & (st[1] > 0), body, (jnp.int32(0), alive(state), state))
    o_ref[0] = jnp.where(lane < SB_HEAD_DIM, state[0][1], state[1][1]).astype(o_ref.dtype)


SB_Q_TILE = 256
SB_LOG_UNDERFLOW = -110.0


def _stick_breaking(qkv, b, s):
    tq = min(SB_Q_TILE, s)
    dim = SB_HEADS * SB_HEAD_DIM
    nb = dim // LANES
    r = jnp.arange(tq)
    upper = (r[:, None] > r[None, :]).astype(BF16)
    o = pl.pallas_call(
        functools.partial(_sb_kernel, tq=tq),
        grid=(b, nb, s // tq),
        in_specs=[pl.BlockSpec((1, tq, LANES), lambda bi, h, i: (bi, i, h)),
                  pl.BlockSpec((1, s, LANES), lambda bi, h, i: (bi, 0, nb + h)),
                  pl.BlockSpec((1, s, LANES), lambda bi, h, i: (bi, 0, 2 * nb + h)),
                  pl.BlockSpec((tq, tq), lambda bi, h, i: (0, 0))],
        out_specs=pl.BlockSpec((1, tq, LANES), lambda bi, h, i: (bi, i, h)),
        out_shape=jax.ShapeDtypeStruct((b, s, dim), BF16),
        compiler_params=_params("arbitrary", "arbitrary", "arbitrary"),
        name="stick_breaking",
    )(qkv, qkv, qkv, upper)
    return o.reshape(b * s, dim)


def _layer_norm(y, g, b):
    mu = jnp.mean(y, axis=-1, keepdims=True)
    d = y - mu
    var = jnp.mean(d * d, axis=-1, keepdims=True)
    return d * lax.rsqrt(var + LN_EPS) * g + b


DISPATCH_SMALL = 48


SUBLANES = 8
DISPATCH_TILE = 256


def _round_up(n, m):
    return -(-n // m) * m


def _region_rows(n_tok):
    n_tiles = -(-n_tok // min(DISPATCH_TILE, n_tok))
    return _round_up(n_tok + (SUBLANES - 1) * n_tiles + MOE_BLOCK, MOE_BLOCK)


def _sorted_rows(tm):
    return _round_up(TOP_K * tm + (SUBLANES - 1) * N_EXPERTS, SUBLANES)


def _proj_ln_dispatch_kernel(h_ref, w_ref, x_ref, g_ref, b_ref, wrh_ref, wrl_ref, br_ref, upper_ref,
                             x1_ref, dest_ref, gate_ref, cnt_ref, xrows_hbm,
                             carry_ref, carry_smem, big_smem, sbuf, sem, *, alpha, region):
    i = pl.program_id(0)
    last = pl.num_programs(0) - 1
    tm = x_ref.shape[0]
    n_sorted = _sorted_rows(tm)

    def aligned(row):
        return row if isinstance(row, int) else pl.multiple_of(row, SUBLANES)

    def segment_copy(src_row, dst_row, nrows):
        return pltpu.make_async_copy(sbuf.at[pl.ds(aligned(src_row), nrows), :],
                                     xrows_hbm.at[pl.ds(aligned(dst_row), nrows), :], sem.at[0])

    def wait_segments():
        for e in range(N_EXPERTS):
            @pl.when(big_smem[e] == 0)
            def _():
                segment_copy(0, 0, DISPATCH_SMALL).wait()

            @pl.when(big_smem[e] != 0)
            def _():
                segment_copy(0, 0, tm).wait()

    @pl.when(i == 0)
    def _():
        carry_ref[...] = jnp.zeros_like(carry_ref)
        sbuf[pl.ds(n_sorted, MOE_BLOCK), :] = jnp.zeros((MOE_BLOCK, sbuf.shape[1]), F32)
        for e in range(N_EXPERTS):
            carry_smem[e] = 0

    y = alpha * x_ref[...] + jnp.dot(h_ref[...], w_ref[...], preferred_element_type=F32)
    x1 = _layer_norm(y, g_ref[...], b_ref[...])
    x1_ref[...] = x1

    xh = x1.astype(BF16)
    xl = (x1 - xh.astype(F32)).astype(BF16)
    wrh = wrh_ref[...]
    logits = (_bdot_nt(wrh, xh) + _bdot_nt(wrh, xl) + _bdot_nt(wrl_ref[...], xh)) + br_ref[...]
    eio = lax.broadcasted_iota(I32, logits.shape, 0)
    neg_inf = jnp.float32(-jnp.inf)
    work = logits
    tops, idxs, sels = [], [], []
    for _ in range(TOP_K):
        m = jnp.max(work, axis=0, keepdims=True)
        idx = jnp.min(jnp.where(work == m, eio, N_EXPERTS), axis=0, keepdims=True)
        sel = eio == idx
        work = jnp.where(sel, neg_inf, work)
        tops.append(m)
        idxs.append(idx)
        sels.append(sel)
    exps = [jnp.exp(m - tops[0]) for m in tops]
    inv_den = 1.0 / functools.reduce(lambda p, r: p + r, exps)
    gate_ref[...] = jnp.concatenate([e * inv_den for e in exps], axis=0)

    onehot = functools.reduce(lambda p, r: p + r, [jnp.where(s, 1.0, 0.0) for s in sels])
    before = jnp.dot(onehot.astype(BF16), upper_ref[...], preferred_element_type=F32)
    counts_row = _bdot_nt(jnp.ones((SUBLANES, tm), BF16), onehot)[0:1, :]
    padded_row = jnp.ceil(counts_row * (1.0 / SUBLANES)) * float(SUBLANES)
    lower = lax.broadcasted_iota(I32, (N_EXPERTS, N_EXPERTS), 1) < lax.broadcasted_iota(I32, (N_EXPERTS, N_EXPERTS), 0)
    seg_start = jnp.sum(jnp.where(lower, padded_row, 0.0), axis=1, keepdims=True)
    carry = carry_ref[...]
    base = lax.broadcasted_iota(I32, carry.shape, 0).astype(F32) * float(region)
    pos = [jnp.sum(jnp.where(s, seg_start + before, 0.0), axis=0, keepdims=True).astype(I32) for s in sels]
    dest = [jnp.sum(jnp.where(s, base + carry + before, 0.0), axis=0, keepdims=True).astype(I32) for s in sels]
    dest_ref[...] = jnp.concatenate(dest, axis=0)
    counts_col = jnp.sum(onehot, axis=1, keepdims=True)
    new_carry = carry + jnp.ceil(counts_col * (1.0 / SUBLANES)) * float(SUBLANES)
    carry_ref[...] = new_carry
    cnt_ref[...] = jnp.broadcast_to(new_carry, cnt_ref.shape)

    jio = lax.broadcasted_iota(I32, (n_sorted, tm), 0)
    hit = functools.reduce(lambda p, r: p | r, [jio == p for p in pos])
    perm = jnp.where(hit, 1.0, 0.0).astype(BF16)

    @pl.when(i > 0)
    def _():
        wait_segments()

    sbuf[pl.ds(0, n_sorted), :] = jnp.dot(perm, xh, preferred_element_type=F32)

    padded_int = padded_row.astype(I32)
    src = 0
    for e in range(N_EXPERTS):
        n_e = padded_int[0, e]
        dst = e * region + carry_smem[e]
        big = n_e > DISPATCH_SMALL
        big_smem[e] = big.astype(I32)

        @pl.when(jnp.logical_not(big))
        def _():
            segment_copy(src, dst, DISPATCH_SMALL).start()

        @pl.when(big)
        def _():
            segment_copy(src, dst, tm).start()

        carry_smem[e] = carry_smem[e] + n_e
        src = src + n_e

    @pl.when(i == last)
    def _():
        wait_segments()
        for e in range(N_EXPERTS):
            segment_copy(n_sorted, e * region + carry_smem[e], MOE_BLOCK).start()
        for e in range(N_EXPERTS):
            segment_copy(0, 0, MOE_BLOCK).wait()


def _proj_ln_dispatch(h, w_out, x, ln_g, ln_b, w_router, b_router, alpha):
    m, kh = h.shape
    d = x.shape[1]
    tm = min(DISPATCH_TILE, m)
    assert tm <= MOE_BLOCK and m % tm == 0
    region = _region_rows(m)
    wrt = w_router.astype(F32).T
    wrh = wrt.astype(BF16)
    wrl = (wrt - wrh.astype(F32)).astype(BF16)
    br = b_router.astype(F32)[:, None]
    r = jnp.arange(tm)
    upper = (r[:, None] < r[None, :]).astype(BF16)
    const = lambda i: (0, 0)
    row = lambda i: (i, 0)
    lane = lambda i: (0, i)
    return pl.pallas_call(
        functools.partial(_proj_ln_dispatch_kernel, alpha=alpha, region=region),
        grid=(m // tm,),
        in_specs=[pl.BlockSpec((tm, kh), row), pl.BlockSpec((kh, d), const), pl.BlockSpec((tm, d), row),
                  pl.BlockSpec((1, d), const), pl.BlockSpec((1, d), const),
                  pl.BlockSpec((N_EXPERTS, d), const), pl.BlockSpec((N_EXPERTS, d), const),
                  pl.BlockSpec((N_EXPERTS, 1), const), pl.BlockSpec((tm, tm), const)],
        out_specs=[pl.BlockSpec((tm, d), row), pl.BlockSpec((TOP_K, tm), lane), pl.BlockSpec((TOP_K, tm), lane),
                   pl.BlockSpec((N_EXPERTS, LANES), const), pl.BlockSpec(memory_space=pl.ANY)],
        out_shape=[jax.ShapeDtypeStruct((m, d), F32), jax.ShapeDtypeStruct((TOP_K, m), I32),
                   jax.ShapeDtypeStruct((TOP_K, m), F32), jax.ShapeDtypeStruct((N_EXPERTS, LANES), F32),
                   jax.ShapeDtypeStruct((N_EXPERTS * region, d), F32)],
        scratch_shapes=[pltpu.VMEM((N_EXPERTS, 1), F32), pltpu.SMEM((N_EXPERTS,), I32), pltpu.SMEM((N_EXPERTS,), I32),
                        pltpu.VMEM((_sorted_rows(tm) + MOE_BLOCK, d), F32),
                        pltpu.SemaphoreType.DMA((1,))],
        compiler_params=_params("arbitrary"),
        name="proj_ln_dispatch",
    )(h, w_out.astype(BF16), x, ln_g.astype(F32)[None, :], ln_b.astype(F32)[None, :], wrh, wrl, br, upper)


def _row_gather_copy(src_hbm, row, dst, dst_row, sem):
    return pltpu.make_async_copy(src_hbm.at[pl.ds(row, 1), :], dst.at[pl.ds(dst_row, 1), :], sem)


GU_GROUP = 2 * LANES


def _deinterleave_kernel(w_ref, p_ref, o_ref):
    w = w_ref[0, 0].astype(BF16)
    perm = p_ref[...]
    for a in range(w.shape[1] // GU_GROUP):
        sl = slice(a * GU_GROUP, (a + 1) * GU_GROUP)
        o_ref[0, :, sl] = jnp.dot(w[:, sl], perm, preferred_element_type=F32).astype(o_ref.dtype)


def _deinterleave_gate_up(w_gu, layer, tk):
    _, e, d, f2 = w_gu.shape
    c = jnp.arange(GU_GROUP)
    src = jnp.where(c < LANES, 2 * c, 2 * (c - LANES) + 1)
    perm = (c[:, None] == src[None, :]).astype(BF16)
    return pl.pallas_call(
        _deinterleave_kernel,
        grid=(e, d // tk),
        in_specs=[pl.BlockSpec((1, 1, tk, f2), lambda ei, ki: (layer, ei, ki, 0)),
                  pl.BlockSpec((GU_GROUP, GU_GROUP), lambda ei, ki: (0, 0))],
        out_specs=pl.BlockSpec((1, tk, f2), lambda ei, ki: (ei, ki, 0)),
        out_shape=jax.ShapeDtypeStruct((e, d, f2), BF16),
        compiler_params=_params("arbitrary", "arbitrary"),
        name="gate_up_regroup",
    )(w_gu, perm)


def _regroup_gate_up_bias(b_gu):
    e, f2 = b_gu.shape
    b4 = b_gu.astype(F32).reshape(e, f2 // GU_GROUP, LANES, 2)
    return jnp.swapaxes(b4, 2, 3).reshape(e, 1, f2)


def _expert_kernel(brow_ref, bexp_ref, nused_ref, x_ref, wgu_ref, bgu_ref, wd_ref, bd_ref, y_ref):
    i = pl.program_id(0)

    @pl.when(i < nused_ref[0])
    def _():
        xb = x_ref[...].astype(BF16)
        hgu = jnp.dot(xb, wgu_ref[0], preferred_element_type=F32) + bgu_ref[0]
        acts = []
        for a in range(hgu.shape[1] // GU_GROUP):
            gate = jnp.minimum(hgu[:, a * GU_GROUP:a * GU_GROUP + LANES], SWIGLU_LIMIT)
            up = jnp.clip(hgu[:, a * GU_GROUP + LANES:(a + 1) * GU_GROUP], -SWIGLU_LIMIT, SWIGLU_LIMIT)
            acts.append(((up + 1.0) * gate * _sigmoid(SWIGLU_ALPHA * gate)).astype(BF16))
        act = jnp.concatenate(acts, axis=1)
        y_ref[...] = jnp.dot(act, wd_ref[0, 0].astype(BF16), preferred_element_type=F32) + bd_ref[0]


def _experts(x_rows, block_row, block_expert, n_used, wgu, bgu, wd_all, layer, bd):
    n_blocks = block_expert.shape[0]
    n_rows, d = x_rows.shape
    f = wd_all.shape[2]
    wspec = lambda shape: pl.BlockSpec(shape, lambda i, br, be, nu: (be[i], 0, 0))
    wd_spec = pl.BlockSpec((1, 1, f, d), lambda i, br, be, nu: (layer, be[i], 0, 0))
    grid_spec = pltpu.PrefetchScalarGridSpec(
        num_scalar_prefetch=3,
        grid=(n_blocks,),
        in_specs=[pl.BlockSpec((MOE_BLOCK, d), lambda i, br, be, nu: (br[i], 0)),
                  wspec((1, d, 2 * f)), wspec((1, 1, 2 * f)), wd_spec, wspec((1, 1, d))],
        out_specs=pl.BlockSpec((MOE_BLOCK, d), lambda i, br, be, nu: (br[i], 0)),
    )
    return pl.pallas_call(
        _expert_kernel,
        grid_spec=grid_spec,
        out_shape=jax.ShapeDtypeStruct((n_rows, d), F32),
        compiler_params=_params("arbitrary"),
        name="moe_experts",
    )(block_row, block_expert, n_used, x_rows, wgu, bgu, wd_all, bd)


def _combine_kernel(d0_ref, d1_ref, y_hbm, gate_ref, x1_ref, g_ref, b_ref, o_ref, ybuf, sem, *, alpha):
    i = pl.program_id(0)
    n = pl.num_programs(0)
    slot = i % 2
    tm = ybuf.shape[2]

    def issue(dest_ref, dst_slot):
        def body(r, c):
            for kk in range(TOP_K):
                _row_gather_copy(y_hbm, dest_ref[0, 0, r * TOP_K + kk], ybuf.at[dst_slot, kk], r,
                                 sem.at[dst_slot]).start()
            return c
        lax.fori_loop(0, tm, body, 0, unroll=2)

    @pl.when(i == 0)
    def _():
        issue(d0_ref, 0)

    for kk in range(TOP_K):
        pltpu.make_async_copy(y_hbm.at[pl.ds(0, tm), :], ybuf.at[slot, kk], sem.at[slot]).wait()

    @pl.when(i + 1 < n)
    def _():
        issue(d1_ref, 1 - slot)

    gates = gate_ref[...]
    f = None
    for kk in range(TOP_K):
        term = gates[:, kk:kk + 1] * ybuf[slot, kk]
        f = term if f is None else f + term
    o_ref[...] = _layer_norm(alpha * x1_ref[...] + f, g_ref[...], b_ref[...])


def _combine(y_rows, dest, gates, x1, ln_g, ln_b, alpha, tm):
    m, d = x1.shape
    tm = min(tm, m)
    nt = m // tm
    dest3 = dest.reshape(nt, 1, tm * TOP_K)
    row = lambda i: (i, 0)
    const = lambda i: (0, 0)
    return pl.pallas_call(
        functools.partial(_combine_kernel, alpha=alpha),
        grid=(nt,),
        in_specs=[pl.BlockSpec((1, 1, tm * TOP_K), lambda i: (i, 0, 0), memory_space=pltpu.SMEM),
                  pl.BlockSpec((1, 1, tm * TOP_K), lambda i: (jnp.minimum(i + 1, nt - 1), 0, 0),
                               memory_space=pltpu.SMEM),
                  pl.BlockSpec(memory_space=pl.ANY),
                  pl.BlockSpec((tm, TOP_K), row), pl.BlockSpec((tm, d), row),
                  pl.BlockSpec((1, d), const), pl.BlockSpec((1, d), const)],
        out_specs=pl.BlockSpec((tm, d), row),
        out_shape=jax.ShapeDtypeStruct((m, d), F32),
        scratch_shapes=[pltpu.VMEM((2, TOP_K, tm, d), F32), pltpu.SemaphoreType.DMA((2,))],
        compiler_params=_params("arbitrary"),
        name="moe_combine",
    )(dest3, dest3, y_rows, gates, x1, ln_g.astype(F32)[None, :], ln_b.astype(F32)[None, :])


def _moe_plan(counts, n_tok):
    cnt = counts[:, 0].astype(I32)
    n_blk = (cnt + MOE_BLOCK - 1) // MOE_BLOCK
    blk_ends = jnp.cumsum(n_blk)
    blk_starts = blk_ends - n_blk
    n_used = blk_ends[-1:]
    n_tiles = -(-n_tok // min(DISPATCH_TILE, n_tok))
    max_rows = n_tok * TOP_K + (SUBLANES - 1) * n_tiles * N_EXPERTS
    n_blocks = -(-max_rows // MOE_BLOCK) + N_EXPERTS
    step = jnp.minimum(jnp.arange(n_blocks, dtype=I32), n_used[0] - 1)
    block_expert = jnp.minimum(
        jnp.sum((blk_ends[None, :] <= step[:, None]).astype(I32), axis=1), N_EXPERTS - 1)
    region_blocks = _region_rows(n_tok) // MOE_BLOCK
    block_row = block_expert * region_blocks + (step - blk_starts[block_expert])
    return block_row.astype(I32), block_expert.astype(I32), n_used.astype(I32)


def kernel(x, gdn_w_in, gdn_conv, gdn_a_log, gdn_dt_bias, gdn_norm_g, gdn_w_out, sb_w_qkv, sb_w_out,
           ln1_g, ln1_b, moe_w_router, moe_b_router, moe_w_gu, moe_b_gu, moe_w_down, moe_b_down,
           ln2_g, ln2_b):
    b, s, d = x.shape
    depth = ln1_g.shape[0]
    alpha = float((2 * depth) ** 0.25)
    xt = x.reshape(b * s, d)
    for i in range(depth):
        j = i // 2
        if i % 2 == 0:
            h = _gated_deltanet(xt, b, s, gdn_w_in[j], gdn_conv[j], gdn_a_log[j], gdn_dt_bias[j],
                                gdn_norm_g[j])
            w_out = gdn_w_out[j]
        else:
            qkv = _matmul(xt, sb_w_qkv[j].astype(BF16), BF16, 1024, 512)
            h = _stick_breaking(qkv.reshape(b, s, -1), b, s)
            w_out = sb_w_out[j]
        x1, dest_t, gates_t, counts, x_rows = _proj_ln_dispatch(
            h, w_out, xt, ln1_g[i], ln1_b[i], moe_w_router[i], moe_b_router[i], alpha)
        block_row, block_expert, n_used = _moe_plan(counts, b * s)
        y_rows = _experts(
            x_rows, block_row, block_expert, n_used,
            _deinterleave_gate_up(moe_w_gu, i, 512), _regroup_gate_up_bias(moe_b_gu[i]),
            moe_w_down, i, moe_b_down[i].astype(F32)[:, None, :])
        xt = _combine(y_rows, dest_t.T, gates_t.T, x1, ln2_g[i], ln2_b[i], alpha, 256)
    return xt.reshape(b, s, d)
```

```python
import functools

import jax
import jax.numpy as jnp
from jax import lax
from jax.experimental import pallas as pl
from jax.experimental.pallas import tpu as pltpu

F32, BF16, I32 = jnp.float32, jnp.bfloat16, jnp.int32

CHUNK = 64
GDN_QK_HEADS = 8
GDN_V_HEADS = 16
GDN_HEAD_DIM = 128
GDN_KEY_DIM = GDN_QK_HEADS * GDN_HEAD_DIM
GDN_VAL_DIM = GDN_V_HEADS * GDN_HEAD_DIM
GDN_CONV_DIM = 2 * GDN_KEY_DIM + GDN_VAL_DIM
SB_HEADS = 16
SB_HEAD_DIM = 64
N_EXPERTS = 32
TOP_K = 4
MOE_BLOCK = 256
SWIGLU_LIMIT = 7.0
SWIGLU_ALPHA = 1.702
LN_EPS = 1e-5
RMS_EPS = 1e-6
L2_EPS = 1e-6

LANES = 128
VMEM_LIMIT_BYTES = 56 * 1024 * 1024


def _params(*sem):
    return pltpu.CompilerParams(dimension_semantics=sem, vmem_limit_bytes=VMEM_LIMIT_BYTES)


def _bdot(a, b):
    return jnp.dot(a.astype(BF16), b.astype(BF16), preferred_element_type=F32)


def _bdot_nt(a, b):
    return lax.dot_general(a.astype(BF16), b.astype(BF16), (((1,), (1,)), ((), ())),
                           preferred_element_type=F32)


def _sigmoid(x):
    return 1.0 / (1.0 + jnp.exp(-x))


def _softplus(x):
    return jnp.maximum(x, 0.0) + jnp.log1p(jnp.exp(-jnp.abs(x)))


def _mm_kernel(x_ref, w_ref, o_ref):
    o_ref[...] = _bdot(x_ref[...], w_ref[...]).astype(o_ref.dtype)


def _matmul(x, w, out_dtype, tm, tn):
    m, k = x.shape
    n = w.shape[1]
    tm, tn = min(tm, m), min(tn, n)
    return pl.pallas_call(
        _mm_kernel,
        grid=(m // tm, n // tn),
        in_specs=[pl.BlockSpec((tm, k), lambda i, j: (i, 0)),
                  pl.BlockSpec((k, tn), lambda i, j: (0, j))],
        out_specs=pl.BlockSpec((tm, tn), lambda i, j: (i, j)),
        out_shape=jax.ShapeDtypeStruct((m, n), out_dtype),
        compiler_params=_params("arbitrary", "arbitrary"),
        name="dense_proj",
    )(x, w)


def _gdn_gate_kernel(x_ref, w_ref, alog_ref, dtb_ref, tri_ref, bg_ref, gc_ref):
    p = _bdot(x_ref[...], w_ref[...])
    col = lax.broadcasted_iota(I32, p.shape, 1)
    beta = _sigmoid(p)
    g = -jnp.exp(alog_ref[...]) * _softplus(p + dtb_ref[...])
    bg = jnp.where(col < GDN_V_HEADS, beta, jnp.where(col < 2 * GDN_V_HEADS, g, 0.0))
    bg_ref[...] = bg
    hi = bg.astype(BF16)
    lo = (bg - hi.astype(F32)).astype(BF16)
    tri = tri_ref[...]
    gc_ref[...] = (jnp.dot(tri, hi, preferred_element_type=F32)
                   + jnp.dot(tri, lo, preferred_element_type=F32))


def _gdn_gates(x, w_ba, a_log, dt_bias, tm):
    m, d = x.shape
    tm = min(tm, m)
    pad = LANES - 2 * GDN_V_HEADS
    w = jnp.pad(w_ba, ((0, 0), (0, pad))).astype(BF16)
    alog = jnp.pad(a_log.astype(F32), (GDN_V_HEADS, pad))[None, :]
    dtb = jnp.pad(dt_bias.astype(F32), (GDN_V_HEADS, pad))[None, :]
    r = jnp.arange(tm)
    tri = ((r[:, None] // CHUNK == r[None, :] // CHUNK) & (r[:, None] >= r[None, :])).astype(BF16)
    return pl.pallas_call(
        _gdn_gate_kernel,
        grid=(m // tm,),
        in_specs=[pl.BlockSpec((tm, d), lambda i: (i, 0)),
                  pl.BlockSpec((d, LANES), lambda i: (0, 0)),
                  pl.BlockSpec((1, LANES), lambda i: (0, 0)),
                  pl.BlockSpec((1, LANES), lambda i: (0, 0)),
                  pl.BlockSpec((tm, tm), lambda i: (0, 0))],
        out_specs=[pl.BlockSpec((tm, LANES), lambda i: (i, 0)),
                   pl.BlockSpec((tm, LANES), lambda i: (i, 0))],
        out_shape=[jax.ShapeDtypeStruct((m, LANES), F32)] * 2,
        compiler_params=_params("arbitrary"),
        name="gdn_gates",
    )(x, w, alog, dtb, tri)


CONV_PREV_ROWS = 16


def _gdn_conv_kernel(cur_ref, prev_ref, w_ref, o_ref, ext_ref, *, tb, n_norm_blocks, n_q_blocks):
    t = pl.program_id(1)
    c = pl.program_id(2)
    prev = prev_ref[0].astype(F32)
    ext_ref[0:CONV_PREV_ROWS, :] = jnp.where(t > 0, prev, 0.0)
    ext_ref[CONV_PREV_ROWS:, :] = cur_ref[0].astype(F32)
    w = w_ref[...]
    taps = w.shape[0]
    acc = None
    for j in range(taps):
        xs = ext_ref[pl.ds(CONV_PREV_ROWS - (taps - 1) + j, tb), :]
        term = xs * w[j:j + 1, :]
        acc = term if acc is None else acc + term
    y = acc * _sigmoid(acc)

    @pl.when(c < n_norm_blocks)
    def _():
        scale = jnp.where(c < n_q_blocks, GDN_HEAD_DIM ** -0.5, 1.0)
        for hsl in range(y.shape[1] // GDN_HEAD_DIM):
            sl = slice(hsl * GDN_HEAD_DIM, (hsl + 1) * GDN_HEAD_DIM)
            yh = y[:, sl]
            inv = lax.rsqrt(jnp.sum(yh * yh, axis=-1, keepdims=True) + L2_EPS) * scale
            o_ref[0, :, sl] = (yh * inv).astype(o_ref.dtype)

    @pl.when(c >= n_norm_blocks)
    def _():
        o_ref[0] = y.astype(o_ref.dtype)


def _gdn_conv(proj, conv_w, tb, cb):
    b, s, _ = proj.shape
    tb = min(tb, s)
    kern = functools.partial(_gdn_conv_kernel, tb=tb, n_norm_blocks=2 * GDN_KEY_DIM // cb,
                             n_q_blocks=GDN_KEY_DIM // cb)
    ratio = tb // CONV_PREV_ROWS
    return pl.pallas_call(
        kern,
        grid=(b, s // tb, GDN_CONV_DIM // cb),
        in_specs=[pl.BlockSpec((1, tb, cb), lambda bi, t, c: (bi, t, c)),
                  pl.BlockSpec((1, CONV_PREV_ROWS, cb),
                               lambda bi, t, c: (bi, jnp.maximum(t * ratio - 1, 0), c)),
                  pl.BlockSpec((conv_w.shape[0], cb), lambda bi, t, c: (0, c))],
        out_specs=pl.BlockSpec((1, tb, cb), lambda bi, t, c: (bi, t, c)),
        out_shape=jax.ShapeDtypeStruct((b, s, GDN_CONV_DIM), BF16),
        scratch_shapes=[pltpu.VMEM((tb + CONV_PREV_ROWS, cb), F32)],
        compiler_params=_params("arbitrary", "arbitrary", "arbitrary"),
        name="gdn_conv",
    )(proj, proj, conv_w.astype(F32))


GDN_KEY_HEADS_PER_STEP = 4


def _gdn_delta_kernel(q_ref, k_ref, v_ref, z_ref, cols_ref, gr_ref, ng_ref, o_ref, s_ref, *, tb):
    @pl.when(pl.program_id(2) == 0)
    def _():
        s_ref[...] = jnp.zeros_like(s_ref)

    hd = GDN_HEAD_DIM
    rep = GDN_V_HEADS // GDN_QK_HEADS
    ri = lax.broadcasted_iota(I32, (tb, tb), 0)
    ci = lax.broadcasted_iota(I32, (tb, tb), 1)
    same64 = (ri >> 6) == (ci >> 6)
    same32 = (ri >> 5) == (ci >> 5)
    same16 = (ri >> 4) == (ci >> 4)
    incl = same64 & (ri >= ci)
    strict = same64 & (ri > ci)
    pair32 = same32 & jnp.logical_not(same16)
    eye = jnp.where(ri == ci, 1.0, 0.0)
    ng = ng_ref[...]

    khs = range(GDN_KEY_HEADS_PER_STEP)
    vhs = range(GDN_KEY_HEADS_PER_STEP * rep)
    each = lambda fn: [fn(vh) for vh in vhs]

    q = [q_ref[0, :, kh * hd:(kh + 1) * hd].astype(F32) for kh in khs]
    k = [k_ref[0, :, kh * hd:(kh + 1) * hd].astype(F32) for kh in khs]
    kk = [_bdot_nt(k[kh], k[kh]) for kh in khs]
    qk = [_bdot_nt(q[kh], k[kh]) for kh in khs]
    cols = cols_ref[0, 0]
    nv = len(vhs)
    row = lax.broadcasted_iota(I32, (tb, 1), 0)
    beta = each(lambda vh: cols[:, vh:vh + 1])
    gc = each(lambda vh: cols[:, nv + vh:nv + vh + 1])

    def chunk_last(col):
        out = col[tb - 1:tb, :]
        for c in range(tb // CHUNK - 2, -1, -1):
            out = jnp.where(row < (c + 1) * CHUNK, col[(c + 1) * CHUNK - 1:(c + 1) * CHUNK, :], out)
        return out

    gl = each(lambda vh: chunk_last(gc[vh]))
    decay = each(lambda vh: jnp.where(incl, jnp.exp(jnp.minimum(gc[vh] - gr_ref[0, vh], 0.0)), 0.0))
    a = each(lambda vh: jnp.where(strict, beta[vh] * kk[vh // rep] * decay[vh], 0.0))
    attn = each(lambda vh: qk[vh // rep] * decay[vh])

    n1 = each(lambda vh: jnp.where(same16, -a[vh], 0.0))
    inv = each(lambda vh: eye + n1[vh])
    n2 = each(lambda vh: _bdot(n1[vh], n1[vh]))
    inv = each(lambda vh: inv[vh] + _bdot(inv[vh], n2[vh]))
    n4 = each(lambda vh: _bdot(n2[vh], n2[vh]))
    inv = each(lambda vh: inv[vh] + _bdot(inv[vh], n4[vh]))
    n8 = each(lambda vh: _bdot(n4[vh], n4[vh]))
    inv = each(lambda vh: inv[vh] + _bdot(inv[vh], n8[vh]))
    t1 = each(lambda vh: _bdot(inv[vh], jnp.where(pair32, a[vh], 0.0)))
    inv = each(lambda vh: inv[vh] - _bdot(t1[vh], inv[vh]))
    t2 = each(lambda vh: _bdot(inv[vh], jnp.where(same32, 0.0, a[vh])))
    inv = each(lambda vh: inv[vh] - _bdot(t2[vh], inv[vh]))

    eg = each(lambda vh: jnp.exp(gc[vh]))
    rhs = each(lambda vh: jnp.concatenate(
        [v_ref[0, :, vh * hd:(vh + 1) * hd].astype(F32) * beta[vh], k[vh // rep] * (beta[vh] * eg[vh])], axis=1))
    uw = each(lambda vh: rhs[vh] + _bdot(inv[vh] - eye, rhs[vh]))
    qd = each(lambda vh: q[vh // rep] * eg[vh])
    kd = each(lambda vh: k[vh // rep] * jnp.exp(gl[vh] - gc[vh]))

    state = each(lambda vh: s_ref[vh])
    outs = [[] for _ in vhs]
    for c in range(tb // CHUNK):
        sl = slice(c * CHUNK, (c + 1) * CHUNK)
        sb = each(lambda vh: state[vh].astype(BF16))
        v_new = each(lambda vh: uw[vh][sl, :hd] - _bdot(uw[vh][sl, hd:], sb[vh]))
        oc = each(lambda vh: _bdot(qd[vh][sl], sb[vh]) + _bdot(attn[vh][sl, sl], v_new[vh]))
        state = each(lambda vh: state[vh] * jnp.exp(gl[vh][c * CHUNK:c * CHUNK + 1, :])
                     + _bdot(kd[vh][sl].T, v_new[vh]))
        for vh in vhs:
            outs[vh].append(oc[vh])
    for vh in vhs:
        s_ref[vh] = state[vh]
        o = jnp.concatenate(outs[vh], axis=0)
        rms = lax.rsqrt(jnp.mean(o * o, axis=-1, keepdims=True) + RMS_EPS)
        zf = z_ref[0, :, vh * hd:(vh + 1) * hd].astype(F32)
        o_ref[0, :, vh * hd:(vh + 1) * hd] = (o * rms * ng * (zf * _sigmoid(zf))).astype(o_ref.dtype)


def _gdn_delta(qkv, z_src, z_col0, cols, gc_row, norm_g, tb):
    b, s, _ = qkv.shape
    hd = GDN_HEAD_DIM
    rep = GDN_V_HEADS // GDN_QK_HEADS
    kw = GDN_KEY_HEADS_PER_STEP * hd
    nv = GDN_KEY_HEADS_PER_STEP * rep
    vw = nv * hd
    return pl.pallas_call(
        functools.partial(_gdn_delta_kernel, tb=tb),
        grid=(b, GDN_QK_HEADS // GDN_KEY_HEADS_PER_STEP, s // tb),
        in_specs=[pl.BlockSpec((1, tb, kw), lambda bi, h, t: (bi, t, h)),
                  pl.BlockSpec((1, tb, kw), lambda bi, h, t: (bi, t, GDN_KEY_DIM // kw + h)),
                  pl.BlockSpec((1, tb, vw), lambda bi, h, t: (bi, t, 2 * GDN_KEY_DIM // vw + h)),
                  pl.BlockSpec((1, tb, vw), lambda bi, h, t: (bi, t, z_col0 // vw + h)),
                  pl.BlockSpec((1, 1, tb, LANES), lambda bi, h, t: (bi, h, t, 0)),
                  pl.BlockSpec((1, nv, 1, tb), lambda bi, h, t: (bi, h, 0, t)),
                  pl.BlockSpec((1, hd), lambda bi, h, t: (0, 0))],
        out_specs=pl.BlockSpec((1, tb, vw), lambda bi, h, t: (bi, t, h)),
        out_shape=jax.ShapeDtypeStruct((b, s, GDN_VAL_DIM), BF16),
        scratch_shapes=[pltpu.VMEM((nv, hd, hd), F32)],
        compiler_params=_params("arbitrary", "arbitrary", "arbitrary"),
        name="gdn_delta",
    )(qkv, qkv, qkv, z_src, cols, gc_row, norm_g.astype(F32)[None, :])


GDN_TIME_BLOCK = 256


def _gated_deltanet(x2d, b, s, w_in, conv_w, a_log, dt_bias, norm_g):
    o1 = GDN_CONV_DIM + GDN_VAL_DIM
    proj = _matmul(x2d, w_in[:, :o1].astype(BF16), BF16, 1024, 512)
    bg, gcum = _gdn_gates(x2d, w_in[:, o1:], a_log, dt_bias, 512)
    proj3 = proj.reshape(b, s, o1)
    qkv = _gdn_conv(proj3, conv_w, 1024, 512)
    nh = GDN_V_HEADS
    nv = GDN_KEY_HEADS_PER_STEP * (GDN_V_HEADS // GDN_QK_HEADS)
    beta = bg[:, :nh].reshape(b, s, nh // nv, nv)
    gc = gcum[:, nh:2 * nh].reshape(b, s, nh // nv, nv)
    cols = jnp.pad(jnp.concatenate([beta, gc], axis=-1), ((0, 0), (0, 0), (0, 0), (0, LANES - 2 * nv)))
    cols = jnp.swapaxes(cols, 1, 2)
    gc_row = jnp.swapaxes(gc.reshape(b, s, nh), 1, 2)[:, :, None, :]
    tb = min(GDN_TIME_BLOCK, s)
    o = _gdn_delta(qkv, proj3, GDN_CONV_DIM, cols, gc_row, norm_g, tb)
    return o.reshape(b * s, GDN_VAL_DIM)


def _sb_kernel(q_ref, k_ref, v_ref, u_ref, o_ref, *, tq):
    i = pl.program_id(2)
    q = q_ref[0] * jnp.asarray(SB_HEAD_DIM ** -0.5, q_ref.dtype)
    lane = lax.broadcasted_iota(I32, q.shape, 1)
    q_heads = (jnp.where(lane < SB_HEAD_DIM, q, jnp.zeros_like(q)),
               jnp.where(lane >= SB_HEAD_DIM, q, jnp.zeros_like(q)))
    upper = u_ref[...]
    ri = lax.broadcasted_iota(I32, (tq, tq), 0)
    ci = lax.broadcasted_iota(I32, (tq, tq), 1)
    causal = ci < ri

    def visit(j, state, diagonal):
        kb = k_ref[0, pl.ds(pl.multiple_of(j * tq, tq), tq), :]
        vb = v_ref[0, pl.ds(pl.multiple_of(j * tq, tq), tq), :]
        heads = range(len(q_heads))
        each = lambda fn: [fn(h) for h in heads]
        z = each(lambda h: lax.dot_general(q_heads[h], kb, (((1,), (1,)), ((), ())), preferred_element_type=F32))
        sp = each(lambda h: _softplus(z[h]))
        log_keep = each(lambda h: jnp.where(causal, -sp[h], 0.0) if diagonal else -sp[h])
        after = each(lambda h: jnp.dot(log_keep[h].astype(BF16), upper, preferred_element_type=F32) + state[h][0])
        wts = each(lambda h: jnp.exp(z[h] - sp[h] + after[h]))
        if diagonal:
            wts = each(lambda h: jnp.where(causal, wts[h], 0.0))
        acc = each(lambda h: state[h][1] + jnp.dot(wts[h].astype(BF16), vb, preferred_element_type=F32))
        carry = each(lambda h: state[h][0] + jnp.sum(log_keep[h], axis=-1, keepdims=True))
        return tuple((carry[h], acc[h]) for h in heads)

    def alive(state):
        top = jnp.maximum(jnp.max(state[0][0]), jnp.max(state[1][0]))
        return (top > SB_LOG_UNDERFLOW).astype(I32)

    def body(st):
        jj, _, state = st
        state = visit(i - 1 - jj, state, False)
        return jj + 1, alive(state), state

    zero = (jnp.zeros((tq, 1), F32), jnp.zeros((tq, LANES), F32))
    state = visit(i, (zero, zero), True)
    _, _, state = lax.while_loop(lambda st: (st[0] < i) & (st[1] > 0), body, (jnp.int32(0), alive(state), state))
    o_ref[0] = jnp.where(lane < SB_HEAD_DIM, state[0][1], state[1][1]).astype(o_ref.dtype)


SB_Q_TILE = 256
SB_LOG_UNDERFLOW = -110.0


def _stick_breaking(qkv, b, s):
    tq = min(SB_Q_TILE, s)
    dim = SB_HEADS * SB_HEAD_DIM
    nb = dim // LANES
    r = jnp.arange(tq)
    upper = (r[:, None] > r[None, :]).astype(BF16)
    o = pl.pallas_call(
        functools.partial(_sb_kernel, tq=tq),
        grid=(b, nb, s // tq),
        in_specs=[pl.BlockSpec((1, tq, LANES), lambda bi, h, i: (bi, i, h)),
                  pl.BlockSpec((1, s, LANES), lambda bi, h, i: (bi, 0, nb + h)),
                  pl.BlockSpec((1, s, LANES), lambda bi, h, i: (bi, 0, 2 * nb + h)),
                  pl.BlockSpec((tq, tq), lambda bi, h, i: (0, 0))],
        out_specs=pl.BlockSpec((1, tq, LANES), lambda bi, h, i: (bi, i, h)),
        out_shape=jax.ShapeDtypeStruct((b, s, dim), BF16),
        compiler_params=_params("arbitrary", "arbitrary", "arbitrary"),
        name="stick_breaking",
    )(qkv, qkv, qkv, upper)
    return o.reshape(b * s, dim)


def _layer_norm(y, g, b):
    mu = jnp.mean(y, axis=-1, keepdims=True)
    d = y - mu
    var = jnp.mean(d * d, axis=-1, keepdims=True)
    return d * lax.rsqrt(var + LN_EPS) * g + b


DISPATCH_SMALL = 48


SUBLANES = 8
DISPATCH_TILE = 256


def _round_up(n, m):
    return -(-n // m) * m


def _region_rows(n_tok):
    n_tiles = -(-n_tok // min(DISPATCH_TILE, n_tok))
    return _round_up(n_tok + (SUBLANES - 1) * n_tiles + MOE_BLOCK, MOE_BLOCK)


def _sorted_rows(tm):
    return _round_up(TOP_K * tm + (SUBLANES - 1) * N_EXPERTS, SUBLANES)


def _proj_ln_dispatch_kernel(h_ref, w_ref, x_ref, g_ref, b_ref, wrh_ref, wrl_ref, br_ref, upper_ref,
                             x1_ref, dest_ref, gate_ref, cnt_ref, xrows_hbm,
                             carry_ref, carry_smem, big_smem, sbuf, sem, *, alpha, region):
    i = pl.program_id(0)
    last = pl.num_programs(0) - 1
    tm = x_ref.shape[0]
    n_sorted = _sorted_rows(tm)

    def aligned(row):
        return row if isinstance(row, int) else pl.multiple_of(row, SUBLANES)

    def segment_copy(src_row, dst_row, nrows):
        return pltpu.make_async_copy(sbuf.at[pl.ds(aligned(src_row), nrows), :],
                                     xrows_hbm.at[pl.ds(aligned(dst_row), nrows), :], sem.at[0])

    def wait_segments():
        for e in range(N_EXPERTS):
            @pl.when(big_smem[e] == 0)
            def _():
                segment_copy(0, 0, DISPATCH_SMALL).wait()

            @pl.when(big_smem[e] != 0)
            def _():
                segment_copy(0, 0, tm).wait()

    @pl.when(i == 0)
    def _():
        carry_ref[...] = jnp.zeros_like(carry_ref)
        sbuf[pl.ds(n_sorted, MOE_BLOCK), :] = jnp.zeros((MOE_BLOCK, sbuf.shape[1]), F32)
        for e in range(N_EXPERTS):
            carry_smem[e] = 0

    y = alpha * x_ref[...] + jnp.dot(h_ref[...], w_ref[...], preferred_element_type=F32)
    x1 = _layer_norm(y, g_ref[...], b_ref[...])
    x1_ref[...] = x1

    xh = x1.astype(BF16)
    xl = (x1 - xh.astype(F32)).astype(BF16)
    wrh = wrh_ref[...]
    logits = (_bdot_nt(wrh, xh) + _bdot_nt(wrh, xl) + _bdot_nt(wrl_ref[...], xh)) + br_ref[...]
    eio = lax.broadcasted_iota(I32, logits.shape, 0)
    neg_inf = jnp.float32(-jnp.inf)
    work = logits
    tops, idxs, sels = [], [], []
    for _ in range(TOP_K):
        m = jnp.max(work, axis=0, keepdims=True)
        idx = jnp.min(jnp.where(work == m, eio, N_EXPERTS), axis=0, keepdims=True)
        sel = eio == idx
        work = jnp.where(sel, neg_inf, work)
        tops.append(m)
        idxs.append(idx)
        sels.append(sel)
    exps = [jnp.exp(m - tops[0]) for m in tops]
    inv_den = 1.0 / functools.reduce(lambda p, r: p + r, exps)
    gate_ref[...] = jnp.concatenate([e * inv_den for e in exps], axis=0)

    onehot = functools.reduce(lambda p, r: p + r, [jnp.where(s, 1.0, 0.0) for s in sels])
    before = jnp.dot(onehot.astype(BF16), upper_ref[...], preferred_element_type=F32)
    counts_row = _bdot_nt(jnp.ones((SUBLANES, tm), BF16), onehot)[0:1, :]
    padded_row = jnp.ceil(counts_row * (1.0 / SUBLANES)) * float(SUBLANES)
    lower = lax.broadcasted_iota(I32, (N_EXPERTS, N_EXPERTS), 1) < lax.broadcasted_iota(I32, (N_EXPERTS, N_EXPERTS), 0)
    seg_start = jnp.sum(jnp.where(lower, padded_row, 0.0), axis=1, keepdims=True)
    carry = carry_ref[...]
    base = lax.broadcasted_iota(I32, carry.shape, 0).astype(F32) * float(region)
    pos = [jnp.sum(jnp.where(s, seg_start + before, 0.0), axis=0, keepdims=True).astype(I32) for s in sels]
    dest = [jnp.sum(jnp.where(s, base + carry + before, 0.0), axis=0, keepdims=True).astype(I32) for s in sels]
    dest_ref[...] = jnp.concatenate(dest, axis=0)
    counts_col = jnp.sum(onehot, axis=1, keepdims=True)
    new_carry = carry + jnp.ceil(counts_col * (1.0 / SUBLANES)) * float(SUBLANES)
    carry_ref[...] = new_carry
    cnt_ref[...] = jnp.broadcast_to(new_carry, cnt_ref.shape)

    jio = lax.broadcasted_iota(I32, (n_sorted, tm), 0)
    hit = functools.reduce(lambda p, r: p | r, [jio == p for p in pos])
    perm = jnp.where(hit, 1.0, 0.0).astype(BF16)

    @pl.when(i > 0)
    def _():
        wait_segments()

    sbuf[pl.ds(0, n_sorted), :] = jnp.dot(perm, xh, preferred_element_type=F32)

    padded_int = padded_row.astype(I32)
    src = 0
    for e in range(N_EXPERTS):
        n_e = padded_int[0, e]
        dst = e * region + carry_smem[e]
        big = n_e > DISPATCH_SMALL
        big_smem[e] = big.astype(I32)

        @pl.when(jnp.logical_not(big))
        def _():
            segment_copy(src, dst, DISPATCH_SMALL).start()

        @pl.when(big)
        def _():
            segment_copy(src, dst, tm).start()

        carry_smem[e] = carry_smem[e] + n_e
        src = src + n_e

    @pl.when(i == last)
    def _():
        wait_segments()
        for e in range(N_EXPERTS):
            segment_copy(n_sorted, e * region + carry_smem[e], MOE_BLOCK).start()
        for e in range(N_EXPERTS):
            segment_copy(0, 0, MOE_BLOCK).wait()


def _proj_ln_dispatch(h, w_out, x, ln_g, ln_b, w_router, b_router, alpha):
    m, kh = h.shape
    d = x.shape[1]
    tm = min(DISPATCH_TILE, m)
    assert tm <= MOE_BLOCK and m % tm == 0
    region = _region_rows(m)
    wrt = w_router.astype(F32).T
    wrh = wrt.astype(BF16)
    wrl = (wrt - wrh.astype(F32)).astype(BF16)
    br = b_router.astype(F32)[:, None]
    r = jnp.arange(tm)
    upper = (r[:, None] < r[None, :]).astype(BF16)
    const = lambda i: (0, 0)
    row = lambda i: (i, 0)
    lane = lambda i: (0, i)
    return pl.pallas_call(
        functools.partial(_proj_ln_dispatch_kernel, alpha=alpha, region=region),
        grid=(m // tm,),
        in_specs=[pl.BlockSpec((tm, kh), row), pl.BlockSpec((kh, d), const), pl.BlockSpec((tm, d), row),
                  pl.BlockSpec((1, d), const), pl.BlockSpec((1, d), const),
                  pl.BlockSpec((N_EXPERTS, d), const), pl.BlockSpec((N_EXPERTS, d), const),
                  pl.BlockSpec((N_EXPERTS, 1), const), pl.BlockSpec((tm, tm), const)],
        out_specs=[pl.BlockSpec((tm, d), row), pl.BlockSpec((TOP_K, tm), lane), pl.BlockSpec((TOP_K, tm), lane),
                   pl.BlockSpec((N_EXPERTS, LANES), const), pl.BlockSpec(memory_space=pl.ANY)],
        out_shape=[jax.ShapeDtypeStruct((m, d), F32), jax.ShapeDtypeStruct((TOP_K, m), I32),
                   jax.ShapeDtypeStruct((TOP_K, m), F32), jax.ShapeDtypeStruct((N_EXPERTS, LANES), F32),
                   jax.ShapeDtypeStruct((N_EXPERTS * region, d), F32)],
        scratch_shapes=[pltpu.VMEM((N_EXPERTS, 1), F32), pltpu.SMEM((N_EXPERTS,), I32), pltpu.SMEM((N_EXPERTS,), I32),
                        pltpu.VMEM((_sorted_rows(tm) + MOE_BLOCK, d), F32),
                        pltpu.SemaphoreType.DMA((1,))],
        compiler_params=_params("arbitrary"),
        name="proj_ln_dispatch",
    )(h, w_out.astype(BF16), x, ln_g.astype(F32)[None, :], ln_b.astype(F32)[None, :], wrh, wrl, br, upper)


def _row_gather_copy(src_hbm, row, dst, dst_row, sem):
    return pltpu.make_async_copy(src_hbm.at[pl.ds(row, 1), :], dst.at[pl.ds(dst_row, 1), :], sem)


GU_GROUP = 2 * LANES


def _deinterleave_kernel(w_ref, p_ref, o_ref):
    w = w_ref[0, 0].astype(BF16)
    perm = p_ref[...]
    for a in range(w.shape[1] // GU_GROUP):
        sl = slice(a * GU_GROUP, (a + 1) * GU_GROUP)
        o_ref[0, :, sl] = jnp.dot(w[:, sl], perm, preferred_element_type=F32).astype(o_ref.dtype)


def _deinterleave_gate_up(w_gu, layer, tk):
    _, e, d, f2 = w_gu.shape
    c = jnp.arange(GU_GROUP)
    src = jnp.where(c < LANES, 2 * c, 2 * (c - LANES) + 1)
    perm = (c[:, None] == src[None, :]).astype(BF16)
    return pl.pallas_call(
        _deinterleave_kernel,
        grid=(e, d // tk),
        in_specs=[pl.BlockSpec((1, 1, tk, f2), lambda ei, ki: (layer, ei, ki, 0)),
                  pl.BlockSpec((GU_GROUP, GU_GROUP), lambda ei, ki: (0, 0))],
        out_specs=pl.BlockSpec((1, tk, f2), lambda ei, ki: (ei, ki, 0)),
        out_shape=jax.ShapeDtypeStruct((e, d, f2), BF16),
        compiler_params=_params("arbitrary", "arbitrary"),
        name="gate_up_regroup",
    )(w_gu, perm)


def _regroup_gate_up_bias(b_gu):
    e, f2 = b_gu.shape
    b4 = b_gu.astype(F32).reshape(e, f2 // GU_GROUP, LANES, 2)
    return jnp.swapaxes(b4, 2, 3).reshape(e, 1, f2)


def _expert_kernel(brow_ref, bexp_ref, nused_ref, x_ref, wgu_ref, bgu_ref, wd_ref, bd_ref, y_ref):
    i = pl.program_id(0)

    @pl.when(i < nused_ref[0])
    def _():
        xb = x_ref[...].astype(BF16)
        hgu = jnp.dot(xb, wgu_ref[0], preferred_element_type=F32) + bgu_ref[0]
        acts = []
        for a in range(hgu.shape[1] // GU_GROUP):
            gate = jnp.minimum(hgu[:, a * GU_GROUP:a * GU_GROUP + LANES], SWIGLU_LIMIT)
            up = jnp.clip(hgu[:, a * GU_GROUP + LANES:(a + 1) * GU_GROUP], -SWIGLU_LIMIT, SWIGLU_LIMIT)
            acts.append(((up + 1.0) * gate * _sigmoid(SWIGLU_ALPHA * gate)).astype(BF16))
        act = jnp.concatenate(acts, axis=1)
        y_ref[...] = jnp.dot(act, wd_ref[0, 0].astype(BF16), preferred_element_type=F32) + bd_ref[0]


def _experts(x_rows, block_row, block_expert, n_used, wgu, bgu, wd_all, layer, bd):
    n_blocks = block_expert.shape[0]
    n_rows, d = x_rows.shape
    f = wd_all.shape[2]
    wspec = lambda shape: pl.BlockSpec(shape, lambda i, br, be, nu: (be[i], 0, 0))
    wd_spec = pl.BlockSpec((1, 1, f, d), lambda i, br, be, nu: (layer, be[i], 0, 0))
    grid_spec = pltpu.PrefetchScalarGridSpec(
        num_scalar_prefetch=3,
        grid=(n_blocks,),
        in_specs=[pl.BlockSpec((MOE_BLOCK, d), lambda i, br, be, nu: (br[i], 0)),
                  wspec((1, d, 2 * f)), wspec((1, 1, 2 * f)), wd_spec, wspec((1, 1, d))],
        out_specs=pl.BlockSpec((MOE_BLOCK, d), lambda i, br, be, nu: (br[i], 0)),
    )
    return pl.pallas_call(
        _expert_kernel,
        grid_spec=grid_spec,
        out_shape=jax.ShapeDtypeStruct((n_rows, d), F32),
        compiler_params=_params("arbitrary"),
        name="moe_experts",
    )(block_row, block_expert, n_used, x_rows, wgu, bgu, wd_all, bd)


def _combine_kernel(d0_ref, d1_ref, y_hbm, gate_ref, x1_ref, g_ref, b_ref, o_ref, ybuf, sem, *, alpha):
    i = pl.program_id(0)
    n = pl.num_programs(0)
    slot = i % 2
    tm = ybuf.shape[2]

    def issue(dest_ref, dst_slot):
        def body(r, c):
            for kk in range(TOP_K):
                _row_gather_copy(y_hbm, dest_ref[0, 0, r * TOP_K + kk], ybuf.at[dst_slot, kk], r,
                                 sem.at[dst_slot]).start()
            return c
        lax.fori_loop(0, tm, body, 0, unroll=2)

    @pl.when(i == 0)
    def _():
        issue(d0_ref, 0)

    for kk in range(TOP_K):
        pltpu.make_async_copy(y_hbm.at[pl.ds(0, tm), :], ybuf.at[slot, kk], sem.at[slot]).wait()

    @pl.when(i + 1 < n)
    def _():
        issue(d1_ref, 1 - slot)

    gates = gate_ref[...]
    f = None
    for kk in range(TOP_K):
        term = gates[:, kk:kk + 1] * ybuf[slot, kk]
        f = term if f is None else f + term
    o_ref[...] = _layer_norm(alpha * x1_ref[...] + f, g_ref[...], b_ref[...])


def _combine(y_rows, dest, gates, x1, ln_g, ln_b, alpha, tm):
    m, d = x1.shape
    tm = min(tm, m)
    nt = m // tm
    dest3 = dest.reshape(nt, 1, tm * TOP_K)
    row = lambda i: (i, 0)
    const = lambda i: (0, 0)
    return pl.pallas_call(
        functools.partial(_combine_kernel, alpha=alpha),
        grid=(nt,),
        in_specs=[pl.BlockSpec((1, 1, tm * TOP_K), lambda i: (i, 0, 0), memory_space=pltpu.SMEM),
                  pl.BlockSpec((1, 1, tm * TOP_K), lambda i: (jnp.minimum(i + 1, nt - 1), 0, 0),
                               memory_space=pltpu.SMEM),
                  pl.BlockSpec(memory_space=pl.ANY),
                  pl.BlockSpec((tm, TOP_K), row), pl.BlockSpec((tm, d), row),
                  pl.BlockSpec((1, d), const), pl.BlockSpec((1, d), const)],
        out_specs=pl.BlockSpec((tm, d), row),
        out_shape=jax.ShapeDtypeStruct((m, d), F32),
        scratch_shapes=[pltpu.VMEM((2, TOP_K, tm, d), F32), pltpu.SemaphoreType.DMA((2,))],
        compiler_params=_params("arbitrary"),
        name="moe_combine",
    )(dest3, dest3, y_rows, gates, x1, ln_g.astype(F32)[None, :], ln_b.astype(F32)[None, :])


def _moe_plan(counts, n_tok):
    cnt = counts[:, 0].astype(I32)
    n_blk = (cnt + MOE_BLOCK - 1) // MOE_BLOCK
    blk_ends = jnp.cumsum(n_blk)
    blk_starts = blk_ends - n_blk
    n_used = blk_ends[-1:]
    n_tiles = -(-n_tok // min(DISPATCH_TILE, n_tok))
    max_rows = n_tok * TOP_K + (SUBLANES - 1) * n_tiles * N_EXPERTS
    n_blocks = -(-max_rows // MOE_BLOCK) + N_EXPERTS
    step = jnp.minimum(jnp.arange(n_blocks, dtype=I32), n_used[0] - 1)
    block_expert = jnp.minimum(
        jnp.sum((blk_ends[None, :] <= step[:, None]).astype(I32), axis=1), N_EXPERTS - 1)
    region_blocks = _region_rows(n_tok) // MOE_BLOCK
    block_row = block_expert * region_blocks + (step - blk_starts[block_expert])
    return block_row.astype(I32), block_expert.astype(I32), n_used.astype(I32)


def kernel(x, gdn_w_in, gdn_conv, gdn_a_log, gdn_dt_bias, gdn_norm_g, gdn_w_out, sb_w_qkv, sb_w_out,
           ln1_g, ln1_b, moe_w_router, moe_b_router, moe_w_gu, moe_b_gu, moe_w_down, moe_b_down,
           ln2_g, ln2_b):
    b, s, d = x.shape
    depth = ln1_g.shape[0]
    alpha = float((2 * depth) ** 0.25)
    xt = x.reshape(b * s, d)
    for i in range(depth):
        j = i // 2
        if i % 2 == 0:
            h = _gated_deltanet(xt, b, s, gdn_w_in[j], gdn_conv[j], gdn_a_log[j], gdn_dt_bias[j],
                                gdn_norm_g[j])
            w_out = gdn_w_out[j]
        else:
            qkv = _matmul(xt, sb_w_qkv[j].astype(BF16), BF16, 1024, 512)
            h = _stick_breaking(qkv.reshape(b, s, -1), b, s)
            w_out = sb_w_out[j]
        x1, dest_t, gates_t, counts, x_rows = _proj_ln_dispatch(
            h, w_out, xt, ln1_g[i], ln1_b[i], moe_w_router[i], moe_b_router[i], alpha)
        block_row, block_expert, n_used = _moe_plan(counts, b * s)
        y_rows = _experts(
            x_rows, block_row, block_expert, n_used,
            _deinterleave_gate_up(moe_w_gu, i, 512), _regroup_gate_up_bias(moe_b_gu[i]),
            moe_w_down, i, moe_b_down[i].astype(F32)[:, None, :])
        xt = _combine(y_rows, dest_t.T, gates_t.T, x1, ln2_g[i], ln2_b[i], alpha, 256)
    return xt.reshape(b, s, d)
```
